```python
import math
import jax, jax.numpy as jnp
from jax import lax
import numpy as np

D_MODEL = 1024
BATCH = 16
SEQ = 4096
DEPTH = 4

BLOCK_Q = 128
ROPE_THETA = 500000.0
ROPE_FRAC = 4
HA = 4
DA = 128
HB = 4
DB = 128
KV_LATENT = 256
HI = 8
DI = 64
TOPK_MAX = 256
HC = 4
DC = 64
DCV = 128
D_FF = 2816
CONV_W = 3
ALPHA = (2 * DEPTH) ** 0.25
BETA = (8 * DEPTH) ** -0.25
LN_EPS = 1e-5
RMS_EPS = 1e-6
IN_WIDTHS = (HA * DA, HA * DA, HA * DA, HA,
             HB * DB, KV_LATENT, HI * DI, DI, HI,
             HC * 2 * DC, HC * 2 * DC, HC * DCV)
D_IN = sum(IN_WIDTHS)

kernel_name = 'fox_dsa_diff_gated_hybrid'


def _layer_norm(x, g, b):
    xf = x.astype(jnp.float32)
    mu = jnp.mean(xf, axis=-1, keepdims=True)
    var = jnp.mean(jnp.square(xf - mu), axis=-1, keepdims=True)
    y = (xf - mu) * lax.rsqrt(var + LN_EPS) * g.astype(jnp.float32) + b.astype(jnp.float32)
    return y.astype(x.dtype)


def _rms_norm(x, g):
    xf = x.astype(jnp.float32)
    y = xf * lax.rsqrt(jnp.mean(xf * xf, axis=-1, keepdims=True) + RMS_EPS) * g.astype(jnp.float32)
    return y.astype(x.dtype)


def _partial_rope(x, pos):
    d = x.shape[-1]
    rot = d // ROPE_FRAC
    half = rot // 2
    inv_freq = ROPE_THETA ** (-jnp.arange(half, dtype=jnp.float32) / half)
    ang = pos.astype(jnp.float32)[..., None] * inv_freq
    cos = jnp.cos(ang)[:, :, None, :]
    sin = jnp.sin(ang)[:, :, None, :]
    xf = x.astype(jnp.float32)
    x1, x2 = xf[..., :half], xf[..., half:rot]
    out = jnp.concatenate([x1 * cos - x2 * sin, x2 * cos + x1 * sin, xf[..., rot:]], axis=-1)
    return out.astype(x.dtype)


def _causal_mask(q0, q1):
    return (q0 + jnp.arange(q1 - q0))[:, None] >= jnp.arange(q1)[None, :]


def _forgetting_attention(q, k, v, log_f):
    B, S, H, D = q.shape
    scale = D ** -0.5
    cum = jnp.cumsum(log_f, axis=1).transpose(0, 2, 1)
    outs = []
    for i in range(S // BLOCK_Q):
        q0, q1 = i * BLOCK_Q, (i + 1) * BLOCK_Q
        s = jnp.einsum('bqhd,bkhd->bhqk', q[:, q0:q1], k[:, :q1],
                       preferred_element_type=jnp.float32) * scale
        s = s + cum[:, :, q0:q1, None] - cum[:, :, None, :q1]
        s = jnp.where(_causal_mask(q0, q1), s, -jnp.inf)
        p = jax.nn.softmax(s, axis=-1).astype(v.dtype)
        outs.append(jnp.einsum('bhqk,bkhd->bqhd', p, v[:, :q1]))
    return jnp.concatenate(outs, axis=1)


def _indexed_sparse_attention(q, k, v, q_idx, k_idx, w_idx, top_k):
    B, S, H, D = q.shape
    scale = D ** -0.5
    gather = jax.vmap(lambda t, i: t[i])
    outs = []
    for i in range(S // BLOCK_Q):
        q0, q1 = i * BLOCK_Q, (i + 1) * BLOCK_Q
        qpos = q0 + jnp.arange(BLOCK_Q)
        r = jax.nn.relu(jnp.einsum('bqhd,bkd->bqhk', q_idx[:, q0:q1], k_idx[:, :q1],
                                   preferred_element_type=jnp.float32) * (DI ** -0.5))
        score = jnp.einsum('bqhk,bqh->bqk', r, w_idx[:, q0:q1].astype(jnp.float32))
        score = jnp.where(_causal_mask(q0, q1), score, -jnp.inf)
        kk = min(top_k, q1)
        _, sel = lax.top_k(score, kk)
        ks = gather(k[:, :q1], sel)
        vs = gather(v[:, :q1], sel)
        s = jnp.einsum('bqhd,bqkd->bhqk', q[:, q0:q1], ks,
                       preferred_element_type=jnp.float32) * scale
        valid = sel <= qpos[None, :, None]
        s = jnp.where(valid[:, None], s, -jnp.inf)
        p = jax.nn.softmax(s, axis=-1).astype(v.dtype)
        outs.append(jnp.einsum('bhqk,bqkd->bqhd', p, vs))
    return jnp.concatenate(outs, axis=1)


def _differential_attention(q, k, v, lam):
    B, S = q.shape[:2]
    scale = q.shape[-1] ** -0.5
    outs = []
    for i in range(S // BLOCK_Q):
        q0, q1 = i * BLOCK_Q, (i + 1) * BLOCK_Q
        s = jnp.einsum('bqhcd,bkhcd->bhcqk', q[:, q0:q1], k[:, :q1],
                       preferred_element_type=jnp.float32) * scale
        s = jnp.where(_causal_mask(q0, q1), s, -jnp.inf)
        p = jax.nn.softmax(s, axis=-1)
        a = (p[:, :, 0] - lam * p[:, :, 1]).astype(v.dtype)
        outs.append(jnp.einsum('bhqk,bkhd->bqhd', a, v[:, :q1]))
    return jnp.concatenate(outs, axis=1)


def _hybrid_mixer(x, pos, w_in, b_f, kv_norm_g, w_ukv, lam_qk, diff_norm_g, w_gate,
                  w_br_a, w_br_b, w_br_c, w_out, lam_init, top_k):
    B, S, _ = x.shape
    splits = np.cumsum(IN_WIDTHS)[:-1].tolist()
    (aq, ak, av, af, bq, bc, biq, bik, biw, cq, ck, cv) = jnp.split(x @ w_in, splits, axis=-1)
    log_f = jax.nn.log_sigmoid(af.astype(jnp.float32) + b_f.astype(jnp.float32))
    o_a = _forgetting_attention(aq.reshape(B, S, HA, DA), ak.reshape(B, S, HA, DA),
                                av.reshape(B, S, HA, DA), log_f).reshape(B, S, HA * DA)
    q_b = _partial_rope(bq.reshape(B, S, HB, DB), pos)
    kv = _rms_norm(bc, kv_norm_g) @ w_ukv
    k_b = _partial_rope(kv[:, :, None, :DB], pos)[:, :, 0]
    v_b = kv[..., DB:]
    qi = _partial_rope(biq.reshape(B, S, HI, DI), pos)
    ki = _partial_rope(bik[:, :, None, :], pos)[:, :, 0]
    o_b = _indexed_sparse_attention(q_b, k_b, v_b, qi, ki, biw * (HI ** -0.5),
                                    top_k).reshape(B, S, HB * DB)
    q_c = _partial_rope(cq.reshape(B, S, 2 * HC, DC), pos).reshape(B, S, HC, 2, DC)
    k_c = _partial_rope(ck.reshape(B, S, 2 * HC, DC), pos).reshape(B, S, HC, 2, DC)
    v_c = cv.reshape(B, S, HC, DCV)
    lq = lam_qk.astype(jnp.float32)
    lam = jnp.exp(jnp.sum(lq[0] * lq[1])) - jnp.exp(jnp.sum(lq[2] * lq[3])) + lam_init
    o_c = _differential_attention(q_c, k_c, v_c, lam)
    o_c = (_rms_norm(o_c, diff_norm_g) * (1.0 - lam_init)).reshape(B, S, HC * DCV)
    g_a, g_b, g_c = jnp.split(jax.nn.sigmoid(x @ w_gate), 3, axis=-1)
    merged = g_a * (o_a @ w_br_a) + g_b * (o_b @ w_br_b) + g_c * (o_c @ w_br_c)
    return merged @ w_out


def _conv_gated_mlp(x, w_up, conv_w, conv_b, w_down):
    gate, val = jnp.split(x @ w_up, 2, axis=-1)
    gate = lax.conv_general_dilated(
        gate, conv_w[:, None, :].astype(gate.dtype), window_strides=(1,),
        padding=((CONV_W - 1, 0),), dimension_numbers=('NWC', 'WIO', 'NWC'),
        feature_group_count=gate.shape[-1]) + conv_b
    return (jax.nn.silu(gate) * val) @ w_down


def setup_inputs(seed: int = 0) -> dict:
    key = jax.random.key(seed)
    ks = jax.random.split(key, 26)
    f32 = jnp.float32

    def nrm(k, shape, fan_in, gain=1.0):
        return jax.random.normal(k, shape, f32) * (gain * fan_in ** -0.5)

    def gain_vec(k, shape):
        return 1.0 + 0.02 * jax.random.normal(k, shape, f32)

    def small(k, shape):
        return 0.02 * jax.random.normal(k, shape, f32)

    x = jax.random.normal(ks[0], (BATCH, SEQ, D_MODEL), f32)
    offs = jax.random.randint(ks[1], (BATCH, 1), 0, SEQ, dtype=jnp.int32)
    positions = offs + jnp.arange(SEQ, dtype=jnp.int32)[None, :]
    b_f = jnp.linspace(2.0, 6.0, HA, dtype=f32)[None, :] + 0.1 * jax.random.normal(ks[4], (DEPTH, HA), f32)
    return {
        'x': x,
        'positions': positions,
        'ln_in_g': gain_vec(ks[2], (D_MODEL,)),
        'ln_in_b': small(ks[3], (D_MODEL,)),
        'w_in': nrm(ks[5], (DEPTH, D_MODEL, D_IN), D_MODEL),
        'b_f': b_f,
        'kv_norm_g': gain_vec(ks[6], (DEPTH, KV_LATENT)),
        'w_ukv': nrm(ks[7], (DEPTH, KV_LATENT, 2 * DB), KV_LATENT),
        'lam_qk': 0.1 * jax.random.normal(ks[8], (DEPTH, 4, DC), f32),
        'diff_norm_g': gain_vec(ks[9], (DEPTH, DCV)),
        'w_gate': nrm(ks[10], (DEPTH, D_MODEL, 3 * D_MODEL), D_MODEL),
        'w_br_a': nrm(ks[11], (DEPTH, HA * DA, D_MODEL), HA * DA, BETA),
        'w_br_b': nrm(ks[12], (DEPTH, HB * DB, D_MODEL), HB * DB, BETA),
        'w_br_c': nrm(ks[13], (DEPTH, HC * DCV, D_MODEL), HC * DCV, BETA),
        'w_out': nrm(ks[14], (DEPTH, D_MODEL, D_MODEL), D_MODEL, BETA),
        'ln1_g': gain_vec(ks[15], (DEPTH, D_MODEL)),
        'ln1_b': small(ks[16], (DEPTH, D_MODEL)),
        'w_up': nrm(ks[17], (DEPTH, D_MODEL, 2 * D_FF), D_MODEL),
        'conv_w': nrm(ks[18], (DEPTH, CONV_W, D_FF), CONV_W),
        'conv_b': small(ks[19], (DEPTH, D_FF)),
        'w_down': nrm(ks[20], (DEPTH, D_FF, D_MODEL), D_FF, BETA),
        'ln2_g': gain_vec(ks[21], (DEPTH, D_MODEL)),
        'ln2_b': small(ks[22], (DEPTH, D_MODEL)),
    }


def reference(x, positions, ln_in_g, ln_in_b, w_in, b_f, kv_norm_g, w_ukv, lam_qk, diff_norm_g,
              w_gate, w_br_a, w_br_b, w_br_c, w_out, ln1_g, ln1_b, w_up, conv_w, conv_b,
              w_down, ln2_g, ln2_b):
    top_k = min(TOPK_MAX, x.shape[1] // 4)
    h = _layer_norm(x, ln_in_g, ln_in_b)
    for l in range(DEPTH):
        lam_init = 0.8 - 0.6 * math.exp(-0.3 * l)
        mix = _hybrid_mixer(h, positions, w_in[l], b_f[l], kv_norm_g[l], w_ukv[l], lam_qk[l],
                            diff_norm_g[l], w_gate[l], w_br_a[l], w_br_b[l], w_br_c[l], w_out[l],
                            lam_init, top_k)
        h = _layer_norm(ALPHA * h + mix, ln1_g[l], ln1_b[l])
        ffn = _conv_gated_mlp(h, w_up[l], conv_w[l], conv_b[l], w_down[l])
        h = _layer_norm(ALPHA * h + ffn, ln2_g[l], ln2_b[l])
    return h
```

```python
import functools
import math

import jax
import jax.numpy as jnp
from jax import lax
from jax.experimental import pallas as pl
from jax.experimental.pallas import tpu as pltpu

D_MODEL = 1024
ROPE_THETA = 500000.0
HA, DA = 4, 128
HB, DB = 4, 128
KV_LATENT = 256
HI, DI = 8, 64
TOPK_MAX = 256
HC, DC, DCV = 4, 64, 128
D_FF = 2816
CONV_W = 3
MODEL_DEPTH = 4
ALPHA = (2 * MODEL_DEPTH) ** 0.25
LN_EPS = 1e-5
RMS_EPS = 1e-6

LANES = 128
SUBLANES = 8
VMEM_LIMIT_BYTES = 56 * 1024 * 1024
MXU_DTYPE = jnp.bfloat16
MASKED = -1e30
INT32_MIN = -2 ** 31
INT32_MAX = 2 ** 31 - 1


def _dot(a, b):
    return jnp.dot(a, b, preferred_element_type=jnp.float32)


def _dot_nt(a, b):
    return lax.dot_general(a, b, (((1,), (1,)), ((), ())), preferred_element_type=jnp.float32)


def _layer_norm_rows(x, g, b):
    mu = jnp.mean(x, axis=-1, keepdims=True)
    xc = x - mu
    var = jnp.mean(xc * xc, axis=-1, keepdims=True)
    return xc * lax.rsqrt(var + LN_EPS) * g + b


def _params(semantics):
    return pltpu.CompilerParams(dimension_semantics=semantics, vmem_limit_bytes=VMEM_LIMIT_BYTES)


def _const_spec(shape):
    nd = len(shape)
    return pl.BlockSpec(shape, lambda *_: (0,) * nd)


def _ln_kernel(x_ref, g_ref, b_ref, o_ref):
    o_ref[...] = _layer_norm_rows(x_ref[...], g_ref[...], b_ref[...])


def _input_layer_norm(x2d, g, b, tm):
    t, d = x2d.shape
    row = pl.BlockSpec((tm, d), lambda i: (i, 0))
    return pl.pallas_call(
        _ln_kernel, grid=(t // tm,), in_specs=[row, _const_spec((1, d)), _const_spec((1, d))],
        out_specs=row, out_shape=jax.ShapeDtypeStruct((t, d), jnp.float32),
        compiler_params=_params(("parallel",)), name="input_ln",
    )(x2d, g.reshape(1, d), b.reshape(1, d))


def _rope_table_kernel(pos_ref, f16_ref, g16_ref, f8_ref, g8_ref, c16_ref, s16_ref, c8_ref, s8_ref):
    p = pos_ref[...]
    a16 = p * f16_ref[...]
    c16_ref[...] = jnp.cos(a16)
    s16_ref[...] = jnp.sin(a16) * g16_ref[...]
    a8 = p * f8_ref[...]
    c8_ref[...] = jnp.cos(a8)
    s8_ref[...] = jnp.sin(a8) * g8_ref[...]


def _lane_freq_sign(head_dim):
    rot = head_dim // 4
    half = rot // 2
    inv_freq = ROPE_THETA ** (-jnp.arange(half, dtype=jnp.float32) / half)
    d = jnp.arange(LANES) % head_dim
    freq = jnp.where(d < rot, inv_freq[d % half], 0.0).astype(jnp.float32)
    sign = jnp.where(d < half, -1.0, jnp.where(d < rot, 1.0, 0.0)).astype(jnp.float32)
    return freq.reshape(1, LANES), sign.reshape(1, LANES)


def _rope_tables(pos_lanes, tm):
    t = pos_lanes.shape[0]
    f16, g16 = _lane_freq_sign(DB)
    f8, g8 = _lane_freq_sign(DI)
    row = pl.BlockSpec((tm, LANES), lambda i: (i, 0))
    vec = _const_spec((1, LANES))
    tab = jax.ShapeDtypeStruct((t, LANES), jnp.float32)
    return pl.pallas_call(
        _rope_table_kernel, grid=(t // tm,), in_specs=[row, vec, vec, vec, vec],
        out_specs=[row] * 4, out_shape=[tab] * 4,
        compiler_params=_params(("parallel",)), name="rope_tables",
    )(pos_lanes, f16, g16, f8, g8)


def _rope_chunk(x, cos, sin_signed, first_half, half):
    partner = jnp.where(first_half, pltpu.roll(x, LANES - half, axis=1), pltpu.roll(x, half, axis=1))
    return x * cos + partner * sin_signed


SMALL_W_OFF, SMALL_F_OFF, SMALL_KI_OFF = 0, HI, LANES - DI


def _proj_kernel(h_ref, c16_ref, s16_ref, c8_ref, s8_ref, wa_ref, wbq_ref, wbc_ref, wiq_ref, wc_ref,
                 wsm_ref, bf_ref, kvg_ref, wukv_ref,
                 qa_ref, ka_ref, va_ref, qb_ref, kb_ref, vb_ref, qi_ref, ki2_ref, sm_ref, cum_ref,
                 qc_ref, kc_ref, vc_ref, carry_ref, *, tiles_per_seq):
    i = pl.program_id(0)
    tm = h_ref.shape[0]
    hb = h_ref[...].astype(MXU_DTYPE)
    lane = lax.broadcasted_iota(jnp.int32, (1, LANES), 1)
    first16 = lane < (DB // 8)
    first8 = (lane % DI) < (DI // 8)
    c16, s16, c8, s8 = c16_ref[...], s16_ref[...], c8_ref[...], s8_ref[...]
    rope16 = lambda v: _rope_chunk(v, c16, s16, first16, DB // 8)
    rope8 = lambda v: _rope_chunk(v, c8, s8, first8, DI // 8)
    odt = qa_ref.dtype

    ya = _dot(hb, wa_ref[...])
    qa_ref[...] = ya[:, :HA * DA].astype(odt)
    ka_ref[...] = ya[:, HA * DA:2 * HA * DA].astype(odt)
    va_ref[...] = ya[:, 2 * HA * DA:].astype(odt)

    yq = _dot(hb, wbq_ref[...])
    for c in range(HB):
        qb_ref[:, c * LANES:(c + 1) * LANES] = rope16(yq[:, c * LANES:(c + 1) * LANES]).astype(odt)
    lat = _dot(hb, wbc_ref[...])
    lat = lat * lax.rsqrt(jnp.mean(lat * lat, axis=-1, keepdims=True) + RMS_EPS) * kvg_ref[...]
    kv = _dot(lat.astype(MXU_DTYPE), wukv_ref[...])
    kb_ref[...] = rope16(kv[:, :DB]).astype(odt)
    vb_ref[...] = kv[:, DB:].astype(odt)

    yi = _dot(hb, wiq_ref[...])
    for c in range(HI * DI // LANES):
        qi_ref[:, c * LANES:(c + 1) * LANES] = (
            rope8(yi[:, c * LANES:(c + 1) * LANES]) * (DI ** -0.5)).astype(odt)

    ysm = _dot(hb, wsm_ref[...])
    sm_ref[...] = ysm * (HI ** -0.5)
    ki = rope8(ysm)
    ki2_ref[...] = jnp.where(lane < SMALL_KI_OFF, pltpu.roll(ki, DI, axis=1), ki).astype(odt)

    z = ysm + bf_ref[...]
    logf = jnp.minimum(z, 0.0) - jnp.log1p(jnp.exp(-jnp.abs(z)))
    r = lax.broadcasted_iota(jnp.int32, (tm, tm), 0)
    cidx = lax.broadcasted_iota(jnp.int32, (tm, tm), 1)
    tri = jnp.where(r >= cidx, 1.0, 0.0).astype(jnp.float32)
    local = jnp.dot(tri, logf, preferred_element_type=jnp.float32, precision=lax.Precision.HIGHEST)

    @pl.when(i % tiles_per_seq == 0)
    def _():
        carry_ref[...] = jnp.zeros_like(carry_ref)

    cum = local + carry_ref[...]
    cum_ref[...] = cum
    carry_ref[...] = cum[tm - 1:tm, :]

    yc = _dot(hb, wc_ref[...])
    nqc = HC * 2 * DC
    for c in range(nqc // LANES):
        qc_ref[:, c * LANES:(c + 1) * LANES] = (
            rope8(yc[:, c * LANES:(c + 1) * LANES]) * (DC ** -0.5)).astype(odt)
        kc_ref[:, c * LANES:(c + 1) * LANES] = rope8(
            yc[:, nqc + c * LANES:nqc + (c + 1) * LANES]).astype(odt)
    vc_ref[...] = yc[:, 2 * nqc:].astype(odt)


def _projections(h, tables, lw, seq, tm):
    t = h.shape[0]
    row = lambda n: pl.BlockSpec((tm, n), lambda i: (i, 0))
    out = lambda n, dt: jax.ShapeDtypeStruct((t, n), dt)
    widths = [(HA * DA, MXU_DTYPE)] * 3 + [(HB * DB, MXU_DTYPE), (DB, MXU_DTYPE), (DB, MXU_DTYPE),
                                           (HI * DI, MXU_DTYPE), (LANES, MXU_DTYPE),
                                           (LANES, jnp.float32), (LANES, jnp.float32)] + \
             [(HC * 2 * DC, MXU_DTYPE)] * 2 + [(HC * DCV, MXU_DTYPE)]
    weights = [lw["wa"], lw["wbq"], lw["wbc"], lw["wiq"], lw["wc"], lw["wsm"], lw["bf"], lw["kvg"],
               lw["wukv"]]
    return pl.pallas_call(
        functools.partial(_proj_kernel, tiles_per_seq=seq // tm),
        grid=(t // tm,),
        in_specs=[row(D_MODEL)] + [row(LANES)] * 4 + [_const_spec(w.shape) for w in weights],
        out_specs=[row(n) for n, _ in widths],
        out_shape=[out(n, dt) for n, dt in widths],
        scratch_shapes=[pltpu.VMEM((1, LANES), jnp.float32)],
        compiler_params=_params(("arbitrary",)), name="projections",
    )(h, *tables, *weights)


def _flash_update(s, v, m, l, acc):
    m_new = jnp.maximum(m, jnp.max(s, axis=-1, keepdims=True))
    alpha = jnp.exp(m - m_new)
    p = jnp.exp(s - m_new)
    l = alpha * l + jnp.sum(p, axis=-1, keepdims=True)
    acc = alpha * acc + _dot(p.astype(MXU_DTYPE), v)
    return m_new, l, acc


def _flash_init(rows, width):
    return (jnp.full((rows, 1), -jnp.inf, jnp.float32), jnp.zeros((rows, 1), jnp.float32),
            jnp.zeros((rows, width), jnp.float32))


def _fox_kernel(q_ref, k_ref, v_ref, c_ref, o_ref, *, scale):
    qi = pl.program_id(2)
    tq = q_ref.shape[1]
    q = q_ref[0]

    def chunk(j, carry, diagonal):
        start = pl.multiple_of(j * tq, tq)
        k = k_ref[0, pl.ds(start, tq), :]
        v = v_ref[0, pl.ds(start, tq), :]
        s = _dot_nt(q, k) * scale - c_ref[0, :, pl.ds(start, tq)]
        if diagonal:
            r = lax.broadcasted_iota(jnp.int32, (tq, tq), 0)
            c = lax.broadcasted_iota(jnp.int32, (tq, tq), 1)
            s = jnp.where(c <= r, s, -jnp.inf)
        return _flash_update(s, v, *carry)

    carry = lax.fori_loop(0, qi, lambda j, cr: chunk(j, cr, False), _flash_init(tq, DA))
    _, l, acc = chunk(qi, carry, True)
    o_ref[0] = (acc / l).astype(o_ref.dtype)


def _fox_attention(qa, ka, va, cum_rows, tq):
    b, s, _ = qa.shape
    qspec = pl.BlockSpec((1, tq, DA), lambda bi, h, i: (bi, i, h))
    kvspec = pl.BlockSpec((1, s, DA), lambda bi, h, i: (bi, 0, h))
    cspec = pl.BlockSpec((1, 1, s), lambda bi, h, i: (bi * HA + h, 0, 0))
    return pl.pallas_call(
        functools.partial(_fox_kernel, scale=DA ** -0.5),
        grid=(b, HA, s // tq), in_specs=[qspec, kvspec, kvspec, cspec], out_specs=qspec,
        out_shape=jax.ShapeDtypeStruct((b, s, HA * DA), MXU_DTYPE),
        compiler_params=_params(("parallel", "parallel", "arbitrary")), name="fox_attention",
    )(qa, ka, va, cum_rows)


def _diff_kernel(q_ref, k_ref, v_ref, lq_ref, g_ref, o_ref, *, lam_init):
    qi = pl.program_id(2)
    tq = q_ref.shape[1]
    q = q_ref[0]
    lane = lax.broadcasted_iota(jnp.int32, (1, 2 * DC), 1)
    zero = jnp.zeros_like(q)
    q2 = jnp.concatenate([jnp.where(lane < DC, q, zero), jnp.where(lane >= DC, q, zero)], axis=0)

    def chunk(j, carry, diagonal):
        start = pl.multiple_of(j * tq, tq)
        k = k_ref[0, pl.ds(start, tq), :]
        v = v_ref[0, pl.ds(start, tq), :]
        s = _dot_nt(q2, k)
        if diagonal:
            r = lax.broadcasted_iota(jnp.int32, (2 * tq, tq), 0)
            r = jnp.where(r >= tq, r - tq, r)
            c = lax.broadcasted_iota(jnp.int32, (2 * tq, tq), 1)
            s = jnp.where(c <= r, s, -jnp.inf)
        return _flash_update(s, v, *carry)

    carry = lax.fori_loop(0, qi, lambda j, cr: chunk(j, cr, False), _flash_init(2 * tq, DCV))
    _, l, acc = chunk(qi, carry, True)
    lq = lq_ref[...]
    lam = (jnp.exp(jnp.sum(lq[0:1] * lq[1:2], axis=-1, keepdims=True))
           - jnp.exp(jnp.sum(lq[2:3] * lq[3:4], axis=-1, keepdims=True)) + lam_init)
    o = acc[:tq] / l[:tq] - lam * (acc[tq:] / l[tq:])
    o = o * lax.rsqrt(jnp.mean(o * o, axis=-1, keepdims=True) + RMS_EPS) * g_ref[...]
    o_ref[0] = (o * (1.0 - lam_init)).astype(o_ref.dtype)


def _diff_attention(qc, kc, vc, lam_qk, norm_g, lam_init, tq):
    b, s, _ = qc.shape
    qspec = pl.BlockSpec((1, tq, DCV), lambda bi, h, i: (bi, i, h))
    kvspec = pl.BlockSpec((1, s, DCV), lambda bi, h, i: (bi, 0, h))
    return pl.pallas_call(
        functools.partial(_diff_kernel, lam_init=lam_init),
        grid=(b, HC, s // tq),
        in_specs=[qspec, kvspec, kvspec, _const_spec((4, DC)), _const_spec((1, DCV))],
        out_specs=qspec, out_shape=jax.ShapeDtypeStruct((b, s, HC * DCV), MXU_DTYPE),
        compiler_params=_params(("parallel", "parallel", "arbitrary")), name="diff_attention",
    )(qc, kc, vc, lam_qk, norm_g.reshape(1, DCV))


def _dsa_kernel(qi_ref, w_ref, ki2_ref, qb_ref, kb_ref, vb_ref, o_ref, key_ref, bias_ref,
                *, top_k, tk, scale):
    qt = pl.program_id(1)
    tq = qi_ref.shape[1]
    seq = key_ref.shape[1]
    sub = tk // LANES
    nch = (qt * tq) // tk + 1
    row_pos = qt * tq + lax.broadcasted_iota(jnp.int32, (tq, 1), 0)
    lane = lax.broadcasted_iota(jnp.int32, (1, LANES), 1)
    col_in_chunk = lax.broadcasted_iota(jnp.int32, (1, tk), 1)

    w = w_ref[0]
    q_heads = []
    for c in range(HI * DI // LANES):
        pair = qi_ref[0, :, c * LANES:(c + 1) * LANES]
        zero = jnp.zeros_like(pair)
        q_heads += [jnp.where(lane < DI, pair, zero), jnp.where(lane >= DI, pair, zero)]
    w_heads = [w[:, SMALL_W_OFF + h:SMALL_W_OFF + h + 1] for h in range(HI)]

    def score_chunk(c, _):
        start = pl.multiple_of(c * tk, tk)
        ki2 = ki2_ref[0, pl.ds(start, tk), :]
        sc = jnp.zeros((tq, tk), jnp.float32)
        for h in range(HI):
            sc = sc + w_heads[h] * jnp.maximum(_dot_nt(q_heads[h], ki2), 0.0)
        sc = jnp.where(sc == 0.0, 0.0, sc)
        sc = jnp.where(start + col_in_chunk <= row_pos, sc, -jnp.inf)
        bits = pltpu.bitcast(sc, jnp.int32)
        key_ref[:, pl.ds(start, tk)] = bits ^ ((bits >> 31) & INT32_MAX)
        return 0

    lax.fori_loop(0, nch, score_chunk, 0)

    def count(pred):
        def body(c, acc):
            start = pl.multiple_of(c * tk, tk)
            ones = jnp.where(pred(key_ref[:, pl.ds(start, tk)], start), 1.0, 0.0)
            for u in range(sub):
                acc = acc + ones[:, u * LANES:(u + 1) * LANES]
            return acc
        acc = lax.fori_loop(0, nch, body, jnp.zeros((tq, LANES), jnp.float32))
        return jnp.sum(acc, axis=-1, keepdims=True)

    k_sel = jnp.minimum(top_k, row_pos + 1).astype(jnp.float32)

    def bisect(_, lohi):
        lo, hi = lohi
        mid = (lo & hi) + ((lo ^ hi) >> 1)
        ok = count(lambda x, start: x >= mid) >= k_sel
        return jnp.where(ok, mid, lo), jnp.where(ok, hi, mid)

    thr, _ = lax.fori_loop(0, 32, bisect, (jnp.full((tq, 1), INT32_MIN, jnp.int32),
                                           jnp.full((tq, 1), INT32_MAX, jnp.int32)))

    n_gt = count(lambda x, start: x > thr)
    n_ge = count(lambda x, start: x >= thr)
    need = k_sel - n_gt
    tie_rows = jnp.max(jnp.where(n_ge - n_gt != need, 1.0, 0.0))

    def tie_cutoff():
        def step(_, lohi):
            lo, hi = lohi
            mid = (lo + hi) >> 1
            got = count(lambda x, start: (x == thr) & (start + col_in_chunk <= mid))
            ok = got >= need
            return jnp.where(ok, lo, mid), jnp.where(ok, mid, hi)
        steps = max(1, math.ceil(math.log2(seq))) + 1
        _, hi = lax.fori_loop(0, steps, step, (jnp.full((tq, 1), -1, jnp.int32),
                                               jnp.full((tq, 1), seq - 1, jnp.int32)))
        return hi

    cutoff = lax.cond(tie_rows > 0.0, tie_cutoff, lambda: jnp.full((tq, 1), seq - 1, jnp.int32))

    def bias_chunk(c, _):
        start = pl.multiple_of(c * tk, tk)
        x = key_ref[:, pl.ds(start, tk)]
        sel = (x > thr) | ((x == thr) & (start + col_in_chunk <= cutoff))
        bias_ref[:, pl.ds(start, tk)] = jnp.where(sel, 0.0, MASKED)
        return 0

    lax.fori_loop(0, nch, bias_chunk, 0)

    q4 = jnp.concatenate([qb_ref[0, :, h * DB:(h + 1) * DB] for h in range(HB)], axis=0)

    def attend(c, carry):
        start = pl.multiple_of(c * tk, tk)
        k = kb_ref[0, pl.ds(start, tk), :]
        v = vb_ref[0, pl.ds(start, tk), :]
        bias = bias_ref[:, pl.ds(start, tk)]
        s = _dot_nt(q4, k) * scale + jnp.concatenate([bias] * HB, axis=0)
        return _flash_update(s, v, *carry)

    _, l, acc = lax.fori_loop(0, nch, attend, _flash_init(HB * tq, DB))
    out = acc / l
    for h in range(HB):
        o_ref[0, :, h * DB:(h + 1) * DB] = out[h * tq:(h + 1) * tq].astype(o_ref.dtype)


def _dsa_attention(qi, sm, ki2, qb, kb, vb, top_k, tq, tk):
    b, s, _ = qb.shape
    qrow = lambda n: pl.BlockSpec((1, tq, n), lambda bi, i: (bi, i, 0))
    full = lambda n: pl.BlockSpec((1, s, n), lambda bi, i: (bi, 0, 0))
    return pl.pallas_call(
        functools.partial(_dsa_kernel, top_k=top_k, tk=tk, scale=DB ** -0.5),
        grid=(b, s // tq),
        in_specs=[qrow(HI * DI), qrow(LANES), full(LANES), qrow(HB * DB), full(DB), full(DB)],
        out_specs=qrow(HB * DB), out_shape=jax.ShapeDtypeStruct((b, s, HB * DB), MXU_DTYPE),
        scratch_shapes=[pltpu.VMEM((tq, s), jnp.int32), pltpu.VMEM((tq, s), jnp.float32)],
        compiler_params=_params(("parallel", "arbitrary")), name="dsa_attention",
    )(qi, sm, ki2, qb, kb, vb)


def _merge_kernel(h_ref, oa_ref, ob_ref, oc_ref, wg_ref, wa_ref, wb_ref, wc_ref, wo_ref, g_ref, b_ref,
                  out_ref):
    h = h_ref[...]
    gates = 1.0 / (1.0 + jnp.exp(-_dot(h.astype(MXU_DTYPE), wg_ref[...])))
    merged = (gates[:, :D_MODEL] * _dot(oa_ref[...], wa_ref[...])
              + gates[:, D_MODEL:2 * D_MODEL] * _dot(ob_ref[...], wb_ref[...])
              + gates[:, 2 * D_MODEL:] * _dot(oc_ref[...], wc_ref[...]))
    mix = _dot(merged.astype(MXU_DTYPE), wo_ref[...])
    out_ref[...] = _layer_norm_rows(ALPHA * h + mix, g_ref[...], b_ref[...])


def _merge(h, oa, ob, oc, lw, tm):
    t = h.shape[0]
    row = lambda n: pl.BlockSpec((tm, n), lambda i: (i, 0))
    weights = [lw["wg"], lw["wbr_a"], lw["wbr_b"], lw["wbr_c"], lw["wo"], lw["ln1_g"], lw["ln1_b"]]
    return pl.pallas_call(
        _merge_kernel, grid=(t // tm,),
        in_specs=[row(D_MODEL), row(HA * DA), row(HB * DB), row(HC * DCV)]
        + [_const_spec(w.shape) for w in weights],
        out_specs=row(D_MODEL), out_shape=jax.ShapeDtypeStruct((t, D_MODEL), jnp.float32),
        compiler_params=_params(("parallel",)), name="merge",
    )(h, oa, ob, oc, *weights)


FF_CHUNK = D_FF // 2
CONV_HALO = SUBLANES


def _ffn_kernel(h_ref, wup_ref, cw_ref, cb_ref, wdn_ref, g_ref, b_ref, out_ref, gate_ref,
                *, tiles_per_seq):
    i = pl.program_id(0)
    tm = h_ref.shape[0]
    h = h_ref[...]
    hb = h.astype(MXU_DTYPE)

    @pl.when(i % tiles_per_seq == 0)
    def _():
        gate_ref[0:CONV_HALO, :] = jnp.zeros((CONV_HALO, D_FF), jnp.float32)

    ffn = jnp.zeros((tm, D_MODEL), jnp.float32)
    for f0 in range(0, D_FF, FF_CHUNK):
        f1 = f0 + FF_CHUNK
        gate_ref[CONV_HALO:, f0:f1] = _dot(hb, wup_ref[:, f0:f1])
        val = _dot(hb, wup_ref[:, D_FF + f0:D_FF + f1])
        conv = cb_ref[:, f0:f1]
        for j in range(CONV_W):
            off = CONV_HALO - (CONV_W - 1) + j
            conv = conv + cw_ref[j:j + 1, f0:f1] * gate_ref[off:off + tm, f0:f1]
        act = conv / (1.0 + jnp.exp(-conv)) * val
        ffn = ffn + _dot(act.astype(MXU_DTYPE), wdn_ref[f0:f1, :])
    gate_ref[0:CONV_HALO, :] = gate_ref[tm:tm + CONV_HALO, :]
    out_ref[...] = _layer_norm_rows(ALPHA * h + ffn, g_ref[...], b_ref[...])


def _conv_gated_mlp(h, lw, seq, tm):
    t = h.shape[0]
    row = pl.BlockSpec((tm, D_MODEL), lambda i: (i, 0))
    weights = [lw["wup"], lw["conv_w"], lw["conv_b"], lw["wdn"], lw["ln2_g"], lw["ln2_b"]]
    return pl.pallas_call(
        functools.partial(_ffn_kernel, tiles_per_seq=seq // tm), grid=(t // tm,),
        in_specs=[row] + [_const_spec(w.shape) for w in weights],
        out_specs=row, out_shape=jax.ShapeDtypeStruct((t, D_MODEL), jnp.float32),
        scratch_shapes=[pltpu.VMEM((tm + CONV_HALO, D_FF), jnp.float32)],
        compiler_params=_params(("arbitrary",)), name="conv_gated_mlp",
    )(h, *weights)


def _layer_weights(l, w_in, b_f, kv_norm_g, w_ukv, w_gate, w_br_a, w_br_b, w_br_c, w_out, ln1_g, ln1_b,
                   w_up, conv_w, conv_b, w_down, ln2_g, ln2_b):
    widths = (HA * DA, HA * DA, HA * DA, HA, HB * DB, KV_LATENT, HI * DI, DI, HI,
              HC * 2 * DC, HC * 2 * DC, HC * DCV)
    offs = [0]
    for n in widths:
        offs.append(offs[-1] + n)
    col = lambda k: w_in[l][:, offs[k]:offs[k + 1]]
    aq, ak, av, af, bq, bc, biq, bik, biw, cq, ck, cv = (col(k) for k in range(len(widths)))
    pad = jnp.zeros((D_MODEL, LANES - HI - HA - DI), w_in.dtype)
    mx = lambda a: a.astype(MXU_DTYPE)
    vec = lambda a: a.reshape(1, -1).astype(jnp.float32)
    bf = jnp.zeros((1, LANES), jnp.float32).at[0, SMALL_F_OFF:SMALL_F_OFF + HA].set(b_f[l])
    return dict(
        wa=mx(jnp.concatenate([aq, ak, av], axis=1)), wbq=mx(bq), wbc=mx(bc), wiq=mx(biq),
        wc=mx(jnp.concatenate([cq, ck, cv], axis=1)),
        wsm=mx(jnp.concatenate([biw, af, pad, bik], axis=1)), bf=bf, kvg=vec(kv_norm_g[l]),
        wukv=mx(w_ukv[l]), wg=mx(w_gate[l]), wbr_a=mx(w_br_a[l]), wbr_b=mx(w_br_b[l]),
        wbr_c=mx(w_br_c[l]), wo=mx(w_out[l]), ln1_g=vec(ln1_g[l]), ln1_b=vec(ln1_b[l]),
        wup=mx(w_up[l]), conv_w=conv_w[l].astype(jnp.float32), conv_b=vec(conv_b[l]),
        wdn=mx(w_down[l]), ln2_g=vec(ln2_g[l]), ln2_b=vec(ln2_b[l]))


def _tile(n, want):
    t = min(n, want)
    assert n % t == 0, (n, t)
    return t


def kernel(x, positions, ln_in_g, ln_in_b, w_in, b_f, kv_norm_g, w_ukv, lam_qk, diff_norm_g, w_gate,
           w_br_a, w_br_b, w_br_c, w_out, ln1_g, ln1_b, w_up, conv_w, conv_b, w_down, ln2_g, ln2_b):
    b, s, d = x.shape
    assert d == D_MODEL and s % LANES == 0
    t = b * s
    depth = w_in.shape[0]
    top_k = min(TOPK_MAX, s // 4)
    tm = _tile(s, 256)
    tq = _tile(s, 256)
    tq_dsa = _tile(s, 128)
    tk_dsa = _tile(s, 512)

    h = _input_layer_norm(x.reshape(t, d), ln_in_g, ln_in_b, tm)
    pos_lanes = jnp.broadcast_to(positions.reshape(t, 1).astype(jnp.float32), (t, LANES))
    tables = _rope_tables(pos_lanes, tm)

    for l in range(depth):
        lam_init = 0.8 - 0.6 * math.exp(-0.3 * l)
        lw = _layer_weights(l, w_in, b_f, kv_norm_g, w_ukv, w_gate, w_br_a, w_br_b, w_br_c, w_out,
                            ln1_g, ln1_b, w_up, conv_w, conv_b, w_down, ln2_g, ln2_b)
        (qa, ka, va, qb, kb, vb, qi, ki2, sm, cum, qc, kc, vc) = _projections(h, tables, lw, s, tm)
        seq3 = lambda a: a.reshape(b, s, a.shape[-1])
        cum_rows = seq3(cum[:, SMALL_F_OFF:SMALL_F_OFF + HA]).transpose(0, 2, 1).reshape(b * HA, 1, s)
        o_a = _fox_attention(seq3(qa), seq3(ka), seq3(va), cum_rows, tq)
        o_b = _dsa_attention(seq3(qi), seq3(sm), seq3(ki2), seq3(qb), seq3(kb), seq3(vb),
                             top_k, tq_dsa, tk_dsa)
        o_c = _diff_attention(seq3(qc), seq3(kc), seq3(vc), lam_qk[l].astype(jnp.float32),
                              diff_norm_g[l].astype(jnp.float32), lam_init, tq)
        h = _merge(h, o_a.reshape(t, -1), o_b.reshape(t, -1), o_c.reshape(t, -1), lw, tm)
        h = _conv_gated_mlp(h, lw, s, tm)
    return h.reshape(b, s, d)
```

```python
import functools
import math

import jax
import jax.numpy as jnp
from jax import lax
from jax.experimental import pallas as pl
from jax.experimental.pallas import tpu as pltpu

D_MODEL = 1024
ROPE_THETA = 500000.0
HA, DA = 4, 128
HB, DB = 4, 128
KV_LATENT = 256
HI, DI = 8, 64
TOPK_MAX = 256
HC, DC, DCV = 4, 64, 128
D_FF = 2816
CONV_W = 3
MODEL_DEPTH = 4
ALPHA = (2 * MODEL_DEPTH) ** 0.25
LN_EPS = 1e-5
RMS_EPS = 1e-6

LANES = 128
SUBLANES = 8
VMEM_LIMIT_BYTES = 56 * 1024 * 1024
MXU_DTYPE = jnp.bfloat16
MASKED = -1e30
LOG2E = math.log2(math.e)
INT32_MIN = -2 ** 31
INT32_MAX = 2 ** 31 - 1


def _dot(a, b):
    return jnp.dot(a, b, preferred_element_type=jnp.float32)


def _layer_norm_rows(x, g, b):
    mu = jnp.mean(x, axis=-1, keepdims=True)
    xc = x - mu
    var = jnp.mean(xc * xc, axis=-1, keepdims=True)
    return xc * lax.rsqrt(var + LN_EPS) * g + b


def _params(semantics):
    return pltpu.CompilerParams(dimension_semantics=semantics, vmem_limit_bytes=VMEM_LIMIT_BYTES)


def _const_spec(shape):
    nd = len(shape)
    return pl.BlockSpec(shape, lambda *_: (0,) * nd)


def _ln_kernel(x_ref, g_ref, b_ref, o_ref):
    o_ref[...] = _layer_norm_rows(x_ref[...], g_ref[...], b_ref[...])


def _input_layer_norm(x2d, g, b, tm):
    t, d = x2d.shape
    row = pl.BlockSpec((tm, d), lambda i: (i, 0))
    return pl.pallas_call(
        _ln_kernel, grid=(t // tm,), in_specs=[row, _const_spec((1, d)), _const_spec((1, d))],
        out_specs=row, out_shape=jax.ShapeDtypeStruct((t, d), jnp.float32),
        compiler_params=_params(("parallel",)), name="input_ln",
    )(x2d, g.reshape(1, d), b.reshape(1, d))


def _rope_table_kernel(pos_ref, f16_ref, g16_ref, f8_ref, g8_ref, c16_ref, s16_ref, c8_ref, s8_ref):
    p = pos_ref[...]
    a16 = p * f16_ref[...]
    c16_ref[...] = jnp.cos(a16)
    s16_ref[...] = jnp.sin(a16) * g16_ref[...]
    a8 = p * f8_ref[...]
    c8_ref[...] = jnp.cos(a8)
    s8_ref[...] = jnp.sin(a8) * g8_ref[...]


def _lane_freq_sign(head_dim):
    rot = head_dim // 4
    half = rot // 2
    inv_freq = ROPE_THETA ** (-jnp.arange(half, dtype=jnp.float32) / half)
    d = jnp.arange(LANES) % head_dim
    freq = jnp.where(d < rot, inv_freq[d % half], 0.0).astype(jnp.float32)
    sign = jnp.where(d < half, -1.0, jnp.where(d < rot, 1.0, 0.0)).astype(jnp.float32)
    return freq.reshape(1, LANES), sign.reshape(1, LANES)


def _rope_tables(pos_lanes, tm):
    t = pos_lanes.shape[0]
    f16, g16 = _lane_freq_sign(DB)
    f8, g8 = _lane_freq_sign(DI)
    row = pl.BlockSpec((tm, LANES), lambda i: (i, 0))
    vec = _const_spec((1, LANES))
    tab = jax.ShapeDtypeStruct((t, LANES), jnp.float32)
    return pl.pallas_call(
        _rope_table_kernel, grid=(t // tm,), in_specs=[row, vec, vec, vec, vec],
        out_specs=[row] * 4, out_shape=[tab] * 4,
        compiler_params=_params(("parallel",)), name="rope_tables",
    )(pos_lanes, f16, g16, f8, g8)


def _rope_chunk(x, cos, sin_signed, first_half, half):
    partner = jnp.where(first_half, pltpu.roll(x, LANES - half, axis=1), pltpu.roll(x, half, axis=1))
    return x * cos + partner * sin_signed


SMALL_W_OFF, SMALL_F_OFF, SMALL_KI_OFF = 0, HI, LANES - DI


def _proj_kernel(h_ref, c16_ref, s16_ref, c8_ref, s8_ref, wa_ref, wbq_ref, wbc_ref, wiq_ref, wc_ref,
                 wsm_ref, bf_ref, kvg_ref, wukv_ref,
                 qa_ref, ka_ref, vat_ref, qb_ref, kb_ref, vbt_ref, qi_ref, ki2_ref, sm_ref,
                 qc_ref, kc_ref, vct_ref, carry_ref, *, tiles_per_seq):
    i = pl.program_id(0)
    tm = h_ref.shape[0]
    hb = h_ref[...].astype(MXU_DTYPE)
    lane = lax.broadcasted_iota(jnp.int32, (1, LANES), 1)
    first16 = lane < (DB // 8)
    first8 = (lane % DI) < (DI // 8)
    c16, s16, c8, s8 = c16_ref[...], s16_ref[...], c8_ref[...], s8_ref[...]
    rope16 = lambda v: _rope_chunk(v, c16, s16, first16, DB // 8)
    rope8 = lambda v: _rope_chunk(v, c8, s8, first8, DI // 8)
    odt = qa_ref.dtype

    ya = _dot(hb, wa_ref[...])
    qa_ref[...] = (ya[:, :HA * DA] * (DA ** -0.5 * LOG2E)).astype(odt)
    for c in range(HA):
        ka_ref[:, 2 * c * DA:(2 * c + 1) * DA] = ya[:, (HA + c) * DA:(HA + c + 1) * DA].astype(odt)
    vat_ref[...] = ya[:, 2 * HA * DA:].T.astype(odt)

    yq = _dot(hb, wbq_ref[...])
    for c in range(HB):
        qb_ref[:, c * LANES:(c + 1) * LANES] = (
            rope16(yq[:, c * LANES:(c + 1) * LANES]) * (DB ** -0.5 * LOG2E)).astype(odt)
    lat = _dot(hb, wbc_ref[...])
    lat = lat * lax.rsqrt(jnp.mean(lat * lat, axis=-1, keepdims=True) + RMS_EPS) * kvg_ref[...]
    kv = _dot(lat.astype(MXU_DTYPE), wukv_ref[...])
    kb_ref[...] = rope16(kv[:, :DB]).astype(odt)
    vbt_ref[...] = kv[:, DB:].T.astype(odt)

    yi = _dot(hb, wiq_ref[...])
    for c in range(HI * DI // LANES):
        qi_ref[:, c * LANES:(c + 1) * LANES] = (
            rope8(yi[:, c * LANES:(c + 1) * LANES]) * (DI ** -0.5)).astype(odt)

    ysm = _dot(hb, wsm_ref[...])
    sm_ref[...] = ysm * (HI ** -0.5)
    ki = rope8(ysm)
    ki2_ref[...] = jnp.where(lane < SMALL_KI_OFF, pltpu.roll(ki, DI, axis=1), ki).astype(odt)

    z = ysm + bf_ref[...]
    logf = jnp.minimum(z, 0.0) - jnp.log1p(jnp.exp(-jnp.abs(z)))
    r = lax.broadcasted_iota(jnp.int32, (tm, tm), 0)
    cidx = lax.broadcasted_iota(jnp.int32, (tm, tm), 1)
    tri = jnp.where(r >= cidx, 1.0, 0.0).astype(jnp.float32)
    local = jnp.dot(tri, logf, preferred_element_type=jnp.float32, precision=lax.Precision.HIGHEST)

    @pl.when(i % tiles_per_seq == 0)
    def _():
        carry_ref[...] = jnp.zeros_like(carry_ref)

    cum = local + carry_ref[...]
    carry_ref[...] = cum[tm - 1:tm, :]
    for c in range(HA):
        bias = cum[:, SMALL_F_OFF + c:SMALL_F_OFF + c + 1] * (-LOG2E)
        hi = bias.astype(odt).astype(jnp.float32)
        mid = (bias - hi).astype(odt).astype(jnp.float32)
        lo = bias - hi - mid
        pieces = jnp.where(lane == 0, hi, jnp.where(lane == 1, mid, jnp.where(lane == 2, lo, 0.0)))
        ka_ref[:, (2 * c + 1) * DA:(2 * c + 2) * DA] = pieces.astype(odt)

    yc = _dot(hb, wc_ref[...])
    nqc = HC * 2 * DC
    for c in range(nqc // LANES):
        qc_ref[:, c * LANES:(c + 1) * LANES] = (
            rope8(yc[:, c * LANES:(c + 1) * LANES]) * (DC ** -0.5 * LOG2E)).astype(odt)
        kc_ref[:, c * LANES:(c + 1) * LANES] = rope8(
            yc[:, nqc + c * LANES:nqc + (c + 1) * LANES]).astype(odt)
    vct_ref[...] = yc[:, 2 * nqc:].T.astype(odt)


def _projections(h, tables, lw, seq, tm):
    t = h.shape[0]
    row = lambda n: pl.BlockSpec((tm, n), lambda i: (i, 0))
    col = lambda n: pl.BlockSpec((n, tm), lambda i: (0, i))
    outs = [(HA * DA, MXU_DTYPE, 0), (HA * 2 * DA, MXU_DTYPE, 0), (HA * DA, MXU_DTYPE, 1),
            (HB * DB, MXU_DTYPE, 0), (DB, MXU_DTYPE, 0), (DB, MXU_DTYPE, 1),
            (HI * DI, MXU_DTYPE, 0), (LANES, MXU_DTYPE, 0), (LANES, jnp.float32, 0),
            (HC * 2 * DC, MXU_DTYPE, 0), (HC * 2 * DC, MXU_DTYPE, 0), (HC * DCV, MXU_DTYPE, 1)]
    weights = [lw["wa"], lw["wbq"], lw["wbc"], lw["wiq"], lw["wc"], lw["wsm"], lw["bf"], lw["kvg"],
               lw["wukv"]]
    return pl.pallas_call(
        functools.partial(_proj_kernel, tiles_per_seq=seq // tm),
        grid=(t // tm,),
        in_specs=[row(D_MODEL)] + [row(LANES)] * 4 + [_const_spec(w.shape) for w in weights],
        out_specs=[col(n) if ax else row(n) for n, _, ax in outs],
        out_shape=[jax.ShapeDtypeStruct((n, t) if ax else (t, n), dt) for n, dt, ax in outs],
        scratch_shapes=[pltpu.VMEM((1, LANES), jnp.float32)],
        compiler_params=_params(("arbitrary",)), name="projections",
    )(h, *tables, *weights)


def _chain_init(n, width, tq):
    return tuple((jnp.full((1, tq), -jnp.inf, jnp.float32), jnp.zeros((1, tq), jnp.float32),
                  jnp.zeros((width, tq), jnp.float32)) for _ in range(n))


STRIP = 32


def _absorb(s_all, vt_all, state, bias_strip=None, diagonal=False):
    stats = []
    for s, (m, l, _) in zip(s_all, state):
        keys, tq = s.shape

        def strip(r0):
            x = s[r0:r0 + STRIP]
            if bias_strip is not None:
                x = x + bias_strip(r0)
            if diagonal:
                kk = r0 + lax.broadcasted_iota(jnp.int32, (STRIP, tq), 0)
                x = jnp.where(kk <= lax.broadcasted_iota(jnp.int32, (STRIP, tq), 1), x, -jnp.inf)
            return x

        fold = lambda x, op: op(x.reshape(STRIP // SUBLANES, SUBLANES, tq), axis=0)
        m_part = fold(strip(0), jnp.max)
        for r0 in range(STRIP, keys, STRIP):
            m_part = jnp.maximum(m_part, fold(strip(r0), jnp.max))
        m_new = jnp.maximum(m, jnp.max(m_part, axis=0, keepdims=True))
        alpha = jnp.exp2(m - m_new)
        l_part = jnp.zeros((SUBLANES, tq), jnp.float32)
        p_strips = []
        for r0 in range(0, keys, STRIP):
            p = jnp.exp2(strip(r0) - m_new)
            l_part = l_part + fold(p, jnp.sum)
            p_strips.append(p.astype(MXU_DTYPE))
        stats.append((m_new, alpha, alpha * l + jnp.sum(l_part, axis=0, keepdims=True),
                      jnp.concatenate(p_strips, axis=0)))
    return tuple((m_new, l, alpha * acc + _dot(vt, p))
                 for (m_new, alpha, l, p), vt, (_, _, acc) in zip(stats, vt_all, state))


def _pipelined_chunks(last, logits, absorb, state):
    def step(j, carry):
        s_all, st = carry
        return logits(j + 1), absorb(j, s_all, st, False)
    s_last, st = lax.fori_loop(0, last, step, (logits(0), state))
    return absorb(last, s_last, st, True)


def _transposed(x):
    return x.astype(jnp.float32).T.astype(MXU_DTYPE)


FOX_HEADS = 4
DIFF_HEADS = 2


def _fox_kernel(q_ref, k_ref, vt_ref, o_ref, *, heads):
    qi = pl.program_id(2)
    tq = q_ref.shape[1]
    ones = jnp.ones((DA, tq), MXU_DTYPE)
    q_t = [jnp.concatenate([_transposed(q_ref[0, :, g * DA:(g + 1) * DA]), ones], axis=0)
           for g in range(heads)]

    def logits(j):
        start = pl.multiple_of(j * tq, tq)
        return tuple(_dot(k_ref[0, pl.ds(start, tq), g * 2 * DA:(g + 1) * 2 * DA], q_t[g])
                     for g in range(heads))

    def absorb(j, s_all, state, diagonal):
        start = pl.multiple_of(j * tq, tq)
        vt_all = [vt_ref[g * DA:(g + 1) * DA, pl.ds(start, tq)] for g in range(heads)]
        return _absorb(s_all, vt_all, state, diagonal=diagonal)

    final = _pipelined_chunks(qi, logits, absorb, _chain_init(heads, DA, tq))
    for g in range(heads):
        _, l, acc = final[g]
        o_ref[0, :, g * DA:(g + 1) * DA] = (acc / l).T.astype(o_ref.dtype)


def _fox_attention(qa, kaug, va_t, tq, heads):
    b, s, _ = qa.shape
    qspec = pl.BlockSpec((1, tq, heads * DA), lambda bi, h, i: (bi, i, h))
    kspec = pl.BlockSpec((1, s, heads * 2 * DA), lambda bi, h, i: (bi, 0, h),
                         pipeline_mode=pl.Buffered(1))
    vspec = pl.BlockSpec((heads * DA, s), lambda bi, h, i: (h, bi), pipeline_mode=pl.Buffered(1))
    return pl.pallas_call(
        functools.partial(_fox_kernel, heads=heads),
        grid=(b, HA // heads, s // tq), in_specs=[qspec, kspec, vspec], out_specs=qspec,
        out_shape=jax.ShapeDtypeStruct((b, s, HA * DA), MXU_DTYPE),
        compiler_params=_params(("parallel", "parallel", "arbitrary")), name="fox_attention",
    )(qa, kaug, va_t)


def _diff_kernel(q_ref, k_ref, vt_ref, lq_ref, g_ref, o_ref, *, heads, lam_init):
    qi = pl.program_id(2)
    tq = q_ref.shape[1]
    row = lax.broadcasted_iota(jnp.int32, (2 * DC, 1), 0)
    q_t = []
    for g in range(heads):
        both = _transposed(q_ref[0, :, g * 2 * DC:(g + 1) * 2 * DC])
        zero = jnp.zeros_like(both)
        q_t += [jnp.where(row < DC, both, zero), jnp.where(row >= DC, both, zero)]

    def logits(j):
        start = pl.multiple_of(j * tq, tq)
        out = []
        for g in range(heads):
            k = k_ref[0, pl.ds(start, tq), g * 2 * DC:(g + 1) * 2 * DC]
            out += [_dot(k, q_t[2 * g]), _dot(k, q_t[2 * g + 1])]
        return tuple(out)

    def absorb(j, s_all, state, diagonal):
        start = pl.multiple_of(j * tq, tq)
        vt_all = [vt_ref[(c // 2) * DCV:(c // 2 + 1) * DCV, pl.ds(start, tq)]
                  for c in range(2 * heads)]
        return _absorb(s_all, vt_all, state, diagonal=diagonal)

    final = _pipelined_chunks(qi, logits, absorb, _chain_init(2 * heads, DCV, tq))
    lq = lq_ref[...]
    lam = (jnp.exp(jnp.sum(lq[0:1] * lq[1:2], axis=-1, keepdims=True))
           - jnp.exp(jnp.sum(lq[2:3] * lq[3:4], axis=-1, keepdims=True)) + lam_init)
    for g in range(heads):
        (_, l1, acc1), (_, l2, acc2) = final[2 * g], final[2 * g + 1]
        o = (acc1 / l1).T - lam * (acc2 / l2).T
        o = o * lax.rsqrt(jnp.mean(o * o, axis=-1, keepdims=True) + RMS_EPS) * g_ref[...]
        o_ref[0, :, g * DCV:(g + 1) * DCV] = (o * (1.0 - lam_init)).astype(o_ref.dtype)


def _diff_attention(qc, kc, vc_t, lam_qk, norm_g, lam_init, tq, heads):
    b, s, _ = qc.shape
    qspec = pl.BlockSpec((1, tq, heads * DCV), lambda bi, h, i: (bi, i, h))
    kspec = pl.BlockSpec((1, s, heads * 2 * DC), lambda bi, h, i: (bi, 0, h),
                         pipeline_mode=pl.Buffered(1))
    vspec = pl.BlockSpec((heads * DCV, s), lambda bi, h, i: (h, bi), pipeline_mode=pl.Buffered(1))
    return pl.pallas_call(
        functools.partial(_diff_kernel, heads=heads, lam_init=lam_init),
        grid=(b, HC // heads, s // tq),
        in_specs=[qspec, kspec, vspec, _const_spec((4, DC)), _const_spec((1, DCV))],
        out_specs=qspec, out_shape=jax.ShapeDtypeStruct((b, s, HC * DCV), MXU_DTYPE),
        compiler_params=_params(("parallel", "parallel", "arbitrary")), name="diff_attention",
    )(qc, kc, vc_t, lam_qk, norm_g.reshape(1, DCV))


COUNT_ROWS = 4 * SUBLANES
BISECT_STEPS_PER_CHECK = 4

def _dsa_kernel(qi_ref, w_ref, ki2_ref, qb_ref, kb_ref, vbt_ref, o_ref, key_ref, bias_ref,
                *, top_k, tk):
    qt = pl.program_id(1)
    tq = qi_ref.shape[1]
    seq = key_ref.shape[0]
    nch = (qt * tq) // tk + 1
    q_pos = qt * tq + lax.broadcasted_iota(jnp.int32, (1, tq), 1)
    k_in_chunk = lax.broadcasted_iota(jnp.int32, (tk, 1), 0)
    row = lax.broadcasted_iota(jnp.int32, (2 * DI, 1), 0)

    w_t = w_ref[0].T
    q_heads = []
    for c in range(HI * DI // LANES):
        pair = _transposed(qi_ref[0, :, c * LANES:(c + 1) * LANES])
        zero = jnp.zeros_like(pair)
        q_heads += [jnp.where(row < DI, pair, zero), jnp.where(row >= DI, pair, zero)]

    def score_chunk(c, _):
        start = pl.multiple_of(c * tk, tk)
        ki2 = ki2_ref[0, pl.ds(start, tk), :]
        sc = jnp.zeros((tk, tq), jnp.float32)
        for h in range(HI):
            sc = sc + w_t[SMALL_W_OFF + h:SMALL_W_OFF + h + 1, :] * jnp.maximum(
                _dot(ki2, q_heads[h]), 0.0)
        sc = jnp.where(sc == 0.0, 0.0, sc)
        sc = jnp.where(start + k_in_chunk <= q_pos, sc, -jnp.inf)
        bits = pltpu.bitcast(sc, jnp.int32)
        key_ref[pl.ds(start, tk), :] = bits ^ ((bits >> 31) & INT32_MAX)
        return 0

    lax.fori_loop(0, nch, score_chunk, 0)

    def count(pred):
        def body(c, acc):
            start = pl.multiple_of(c * tk, tk)
            ones = jnp.where(pred(key_ref[pl.ds(start, tk), :], start), 1.0, 0.0)
            return acc + jnp.sum(ones.reshape(tk // COUNT_ROWS, COUNT_ROWS, tq), axis=0)
        acc = lax.fori_loop(0, nch, body, jnp.zeros((COUNT_ROWS, tq), jnp.float32))
        return jnp.sum(acc, axis=0, keepdims=True)

    k_sel = jnp.minimum(top_k, q_pos + 1).astype(jnp.float32)
    total = (nch * tk).astype(jnp.float32)

    n_pos = count(lambda x, start: x >= 1)
    n_nonneg = count(lambda x, start: x >= 0)
    pos, neg = n_pos >= k_sel, n_nonneg < k_sel
    lo = jnp.where(pos, 1, jnp.where(neg, INT32_MIN, 0)).astype(jnp.int32)
    hi = jnp.where(pos, INT32_MAX, jnp.where(neg, 0, 1)).astype(jnp.int32)
    n_lo = jnp.where(pos, n_pos, jnp.where(neg, total, n_nonneg))

    def unsettled(lo, hi, n_lo):
        return jnp.max(jnp.where((n_lo != k_sel) & (hi != lo + 1), 1.0, 0.0)) > 0.0

    def bisect(carry):
        _, lo, hi, n_lo = carry
        for _ in range(BISECT_STEPS_PER_CHECK):
            open_ = (n_lo != k_sel) & (hi != lo + 1)
            mid = (lo & hi) + ((lo ^ hi) >> 1)
            n_mid = count(lambda x, start: x >= mid)
            up = open_ & (n_mid >= k_sel)
            down = open_ & (n_mid < k_sel)
            lo, hi, n_lo = (jnp.where(up, mid, lo), jnp.where(down, mid, hi),
                            jnp.where(up, n_mid, n_lo))
        return unsettled(lo, hi, n_lo), lo, hi, n_lo

    _, thr, _, n_ge = lax.while_loop(lambda carry: carry[0], bisect,
                                     (unsettled(lo, hi, n_lo), lo, hi, n_lo))

    def tie_cutoff():
        need = k_sel - count(lambda x, start: x > thr)

        def step(_, lohi):
            lo, hi = lohi
            mid = (lo + hi) >> 1
            got = count(lambda x, start: (x == thr) & (start + k_in_chunk <= mid))
            ok = got >= need
            return jnp.where(ok, lo, mid), jnp.where(ok, mid, hi)
        steps = max(1, math.ceil(math.log2(seq))) + 1
        _, hi = lax.fori_loop(0, steps, step, (jnp.full((1, tq), -1, jnp.int32),
                                               jnp.full((1, tq), seq - 1, jnp.int32)))
        return hi

    any_tie = jnp.max(jnp.where(n_ge != k_sel, 1.0, 0.0)) > 0.0
    cutoff = lax.cond(any_tie, tie_cutoff, lambda: jnp.full((1, tq), seq - 1, jnp.int32))

    def bias_chunk(c, _):
        start = pl.multiple_of(c * tk, tk)
        x = key_ref[pl.ds(start, tk), :]
        sel = (x > thr) | ((x == thr) & (start + k_in_chunk <= cutoff))
        bias_ref[pl.ds(start, tk), :] = jnp.where(sel, 0.0, MASKED)
        return 0

    lax.fori_loop(0, nch, bias_chunk, 0)

    qb_t = [_transposed(qb_ref[0, :, h * DB:(h + 1) * DB]) for h in range(HB)]

    def logits(c):
        start = pl.multiple_of(c * tk, tk)
        k = kb_ref[0, pl.ds(start, tk), :]
        return tuple(_dot(k, qb_t[h]) for h in range(HB))

    def absorb(c, s_all, state, _):
        start = pl.multiple_of(c * tk, tk)
        vt = vbt_ref[:, pl.ds(start, tk)]
        return _absorb(s_all, [vt] * HB, state,
                       bias_strip=lambda r0: bias_ref[pl.ds(start + r0, STRIP), :])

    final = _pipelined_chunks(nch - 1, logits, absorb, _chain_init(HB, DB, tq))
    for h in range(HB):
        _, l, acc = final[h]
        o_ref[0, :, h * DB:(h + 1) * DB] = (acc / l).T.astype(o_ref.dtype)


def _dsa_attention(qi, sm, ki2, qb, kb, vb_t, top_k, tq, tk):
    b, s, _ = qb.shape
    qrow = lambda n: pl.BlockSpec((1, tq, n), lambda bi, i: (bi, i, 0))
    full = lambda n: pl.BlockSpec((1, s, n), lambda bi, i: (bi, 0, 0), pipeline_mode=pl.Buffered(1))
    vspec = pl.BlockSpec((DB, s), lambda bi, i: (0, bi), pipeline_mode=pl.Buffered(1))
    return pl.pallas_call(
        functools.partial(_dsa_kernel, top_k=top_k, tk=tk),
        grid=(b, s // tq),
        in_specs=[qrow(HI * DI), qrow(LANES), full(LANES), qrow(HB * DB), full(DB), vspec],
        out_specs=qrow(HB * DB), out_shape=jax.ShapeDtypeStruct((b, s, HB * DB), MXU_DTYPE),
        scratch_shapes=[pltpu.VMEM((s, tq), jnp.int32), pltpu.VMEM((s, tq), jnp.float32)],
        compiler_params=_params(("parallel", "arbitrary")), name="dsa_attention",
    )(qi, sm, ki2, qb, kb, vb_t)


def _merge_kernel(h_ref, oa_ref, ob_ref, oc_ref, wg_ref, wa_ref, wb_ref, wc_ref, wo_ref, g_ref, b_ref,
                  out_ref):
    h = h_ref[...]
    gates = 1.0 / (1.0 + jnp.exp(-_dot(h.astype(MXU_DTYPE), wg_ref[...])))
    merged = (gates[:, :D_MODEL] * _dot(oa_ref[...], wa_ref[...])
              + gates[:, D_MODEL:2 * D_MODEL] * _dot(ob_ref[...], wb_ref[...])
              + gates[:, 2 * D_MODEL:] * _dot(oc_ref[...], wc_ref[...]))
    mix = _dot(merged.astype(MXU_DTYPE), wo_ref[...])
    out_ref[...] = _layer_norm_rows(ALPHA * h + mix, g_ref[...], b_ref[...])


def _merge(h, oa, ob, oc, lw, tm):
    t = h.shape[0]
    row = lambda n: pl.BlockSpec((tm, n), lambda i: (i, 0))
    weights = [lw["wg"], lw["wbr_a"], lw["wbr_b"], lw["wbr_c"], lw["wo"], lw["ln1_g"], lw["ln1_b"]]
    return pl.pallas_call(
        _merge_kernel, grid=(t // tm,),
        in_specs=[row(D_MODEL), row(HA * DA), row(HB * DB), row(HC * DCV)]
        + [_const_spec(w.shape) for w in weights],
        out_specs=row(D_MODEL), out_shape=jax.ShapeDtypeStruct((t, D_MODEL), jnp.float32),
        compiler_params=_params(("parallel",)), name="merge",
    )(h, oa, ob, oc, *weights)


FF_CHUNK = D_FF // 2
CONV_HALO = SUBLANES


def _ffn_kernel(h_ref, wup_ref, cw_ref, cb_ref, wdn_ref, g_ref, b_ref, out_ref, gate_ref,
                *, tiles_per_seq):
    i = pl.program_id(0)
    tm = h_ref.shape[0]
    h = h_ref[...]
    hb = h.astype(MXU_DTYPE)

    @pl.when(i % tiles_per_seq == 0)
    def _():
        gate_ref[0:CONV_HALO, :] = jnp.zeros((CONV_HALO, D_FF), jnp.float32)

    ffn = jnp.zeros((tm, D_MODEL), jnp.float32)
    for f0 in range(0, D_FF, FF_CHUNK):
        f1 = f0 + FF_CHUNK
        gate_ref[CONV_HALO:, f0:f1] = _dot(hb, wup_ref[:, f0:f1])
        val = _dot(hb, wup_ref[:, D_FF + f0:D_FF + f1])
        conv = cb_ref[:, f0:f1]
        for j in range(CONV_W):
            off = CONV_HALO - (CONV_W - 1) + j
            conv = conv + cw_ref[j:j + 1, f0:f1] * gate_ref[off:off + tm, f0:f1]
        act = conv / (1.0 + jnp.exp(-conv)) * val
        ffn = ffn + _dot(act.astype(MXU_DTYPE), wdn_ref[f0:f1, :])
    gate_ref[0:CONV_HALO, :] = gate_ref[tm:tm + CONV_HALO, :]
    out_ref[...] = _layer_norm_rows(ALPHA * h + ffn, g_ref[...], b_ref[...])


def _conv_gated_mlp(h, lw, seq, tm):
    t = h.shape[0]
    row = pl.BlockSpec((tm, D_MODEL), lambda i: (i, 0))
    weights = [lw["wup"], lw["conv_w"], lw["conv_b"], lw["wdn"], lw["ln2_g"], lw["ln2_b"]]
    return pl.pallas_call(
        functools.partial(_ffn_kernel, tiles_per_seq=seq // tm), grid=(t // tm,),
        in_specs=[row] + [_const_spec(w.shape) for w in weights],
        out_specs=row, out_shape=jax.ShapeDtypeStruct((t, D_MODEL), jnp.float32),
        scratch_shapes=[pltpu.VMEM((tm + CONV_HALO, D_FF), jnp.float32)],
        compiler_params=_params(("arbitrary",)), name="conv_gated_mlp",
    )(h, *weights)


def _layer_weights(l, w_in, b_f, kv_norm_g, w_ukv, w_gate, w_br_a, w_br_b, w_br_c, w_out, ln1_g, ln1_b,
                   w_up, conv_w, conv_b, w_down, ln2_g, ln2_b):
    widths = (HA * DA, HA * DA, HA * DA, HA, HB * DB, KV_LATENT, HI * DI, DI, HI,
              HC * 2 * DC, HC * 2 * DC, HC * DCV)
    offs = [0]
    for n in widths:
        offs.append(offs[-1] + n)
    col = lambda k: w_in[l][:, offs[k]:offs[k + 1]]
    aq, ak, av, af, bq, bc, biq, bik, biw, cq, ck, cv = (col(k) for k in range(len(widths)))
    pad = jnp.zeros((D_MODEL, LANES - HI - HA - DI), w_in.dtype)
    mx = lambda a: a.astype(MXU_DTYPE)
    vec = lambda a: a.reshape(1, -1).astype(jnp.float32)
    bf = jnp.zeros((1, LANES), jnp.float32).at[0, SMALL_F_OFF:SMALL_F_OFF + HA].set(b_f[l])
    return dict(
        wa=mx(jnp.concatenate([aq, ak, av], axis=1)), wbq=mx(bq), wbc=mx(bc), wiq=mx(biq),
        wc=mx(jnp.concatenate([cq, ck, cv], axis=1)),
        wsm=mx(jnp.concatenate([biw, af, pad, bik], axis=1)), bf=bf, kvg=vec(kv_norm_g[l]),
        wukv=mx(w_ukv[l]), wg=mx(w_gate[l]), wbr_a=mx(w_br_a[l]), wbr_b=mx(w_br_b[l]),
        wbr_c=mx(w_br_c[l]), wo=mx(w_out[l]), ln1_g=vec(ln1_g[l]), ln1_b=vec(ln1_b[l]),
        wup=mx(w_up[l]), conv_w=conv_w[l].astype(jnp.float32), conv_b=vec(conv_b[l]),
        wdn=mx(w_down[l]), ln2_g=vec(ln2_g[l]), ln2_b=vec(ln2_b[l]))


def _tile(n, want):
    t = min(n, want)
    assert n % t == 0, (n, t)
    return t


def kernel(x, positions, ln_in_g, ln_in_b, w_in, b_f, kv_norm_g, w_ukv, lam_qk, diff_norm_g, w_gate,
           w_br_a, w_br_b, w_br_c, w_out, ln1_g, ln1_b, w_up, conv_w, conv_b, w_down, ln2_g, ln2_b):
    b, s, d = x.shape
    assert d == D_MODEL and s % LANES == 0
    t = b * s
    depth = w_in.shape[0]
    top_k = min(TOPK_MAX, s // 4)
    tm = _tile(s, 256)
    tq = _tile(s, 256)
    tk_dsa = _tile(s, 512)

    h = _input_layer_norm(x.reshape(t, d), ln_in_g, ln_in_b, tm)
    pos_lanes = jnp.broadcast_to(positions.reshape(t, 1).astype(jnp.float32), (t, LANES))
    tables = _rope_tables(pos_lanes, tm)

    for l in range(depth):
        lam_init = 0.8 - 0.6 * math.exp(-0.3 * l)
        lw = _layer_weights(l, w_in, b_f, kv_norm_g, w_ukv, w_gate, w_br_a, w_br_b, w_br_c, w_out,
                            ln1_g, ln1_b, w_up, conv_w, conv_b, w_down, ln2_g, ln2_b)
        (qa, ka, va_t, qb, kb, vb_t, qi, ki2, sm, qc, kc, vc_t) = _projections(h, tables, lw, s, tm)
        seq3 = lambda a: a.reshape(b, s, a.shape[-1])
        o_a = _fox_attention(seq3(qa), seq3(ka), va_t, tq, FOX_HEADS)
        o_b = _dsa_attention(seq3(qi), seq3(sm), seq3(ki2), seq3(qb), seq3(kb), vb_t,
                             top_k, tq, tk_dsa)
        o_c = _diff_attention(seq3(qc), seq3(kc), vc_t, lam_qk[l].astype(jnp.float32),
                              diff_norm_g[l].astype(jnp.float32), lam_init, tq, DIFF_HEADS)
        h = _merge(h, o_a.reshape(t, -1), o_b.reshape(t, -1), o_c.reshape(t, -1), lw, tm)
        h = _conv_gated_mlp(h, lw, s, tm)
    return h.reshape(b, s, d)
```

```python
import functools
import math

import jax
import jax.numpy as jnp
from jax import lax
from jax.experimental import pallas as pl
from jax.experimental.pallas import tpu as pltpu

D_MODEL = 1024
ROPE_THETA = 500000.0
HA, DA = 4, 128
HB, DB = 4, 128
KV_LATENT = 256
HI, DI = 8, 64
TOPK_MAX = 256
HC, DC, DCV = 4, 64, 128
D_FF = 2816
CONV_W = 3
MODEL_DEPTH = 4
ALPHA = (2 * MODEL_DEPTH) ** 0.25
LN_EPS = 1e-5
RMS_EPS = 1e-6

LANES = 128
SUBLANES = 8
VMEM_LIMIT_BYTES = 56 * 1024 * 1024
MXU_DTYPE = jnp.bfloat16
MASKED = -1e30
LOG2E = math.log2(math.e)
INT32_MIN = -2 ** 31
INT32_MAX = 2 ** 31 - 1


def _dot(a, b):
    return jnp.dot(a, b, preferred_element_type=jnp.float32)


def _layer_norm_rows(x, g, b):
    mu = jnp.mean(x, axis=-1, keepdims=True)
    xc = x - mu
    var = jnp.mean(xc * xc, axis=-1, keepdims=True)
    return xc * lax.rsqrt(var + LN_EPS) * g + b


def _params(semantics):
    return pltpu.CompilerParams(dimension_semantics=semantics, vmem_limit_bytes=VMEM_LIMIT_BYTES)


def _const_spec(shape):
    nd = len(shape)
    return pl.BlockSpec(shape, lambda *_: (0,) * nd, pipeline_mode=pl.Buffered(1))


def _ln_kernel(x_ref, g_ref, b_ref, o_ref):
    o_ref[...] = _layer_norm_rows(x_ref[...], g_ref[...], b_ref[...])


def _input_layer_norm(x2d, g, b, tm):
    t, d = x2d.shape
    row = pl.BlockSpec((tm, d), lambda i: (i, 0))
    return pl.pallas_call(
        _ln_kernel, grid=(t // tm,), in_specs=[row, _const_spec((1, d)), _const_spec((1, d))],
        out_specs=row, out_shape=jax.ShapeDtypeStruct((t, d), jnp.float32),
        compiler_params=_params(("parallel",)), name="input_ln",
    )(x2d, g.reshape(1, d), b.reshape(1, d))


def _rope_table_kernel(pos_ref, f16_ref, g16_ref, f8_ref, g8_ref, c16_ref, s16_ref, c8_ref, s8_ref):
    p = pos_ref[...]
    a16 = p * f16_ref[...]
    c16_ref[...] = jnp.cos(a16)
    s16_ref[...] = jnp.sin(a16) * g16_ref[...]
    a8 = p * f8_ref[...]
    c8_ref[...] = jnp.cos(a8)
    s8_ref[...] = jnp.sin(a8) * g8_ref[...]


def _lane_freq_sign(head_dim):
    rot = head_dim // 4
    half = rot // 2
    inv_freq = ROPE_THETA ** (-jnp.arange(half, dtype=jnp.float32) / half)
    d = jnp.arange(LANES) % head_dim
    freq = jnp.where(d < rot, inv_freq[d % half], 0.0).astype(jnp.float32)
    sign = jnp.where(d < half, -1.0, jnp.where(d < rot, 1.0, 0.0)).astype(jnp.float32)
    return freq.reshape(1, LANES), sign.reshape(1, LANES)


def _rope_tables(pos_lanes, tm):
    t = pos_lanes.shape[0]
    f16, g16 = _lane_freq_sign(DB)
    f8, g8 = _lane_freq_sign(DI)
    row = pl.BlockSpec((tm, LANES), lambda i: (i, 0))
    vec = _const_spec((1, LANES))
    tab = jax.ShapeDtypeStruct((t, LANES), jnp.float32)
    return pl.pallas_call(
        _rope_table_kernel, grid=(t // tm,), in_specs=[row, vec, vec, vec, vec],
        out_specs=[row] * 4, out_shape=[tab] * 4,
        compiler_params=_params(("parallel",)), name="rope_tables",
    )(pos_lanes, f16, g16, f8, g8)


def _rope_chunk(x, cos, sin_signed, first_half, half):
    partner = jnp.where(first_half, pltpu.roll(x, LANES - half, axis=1), pltpu.roll(x, half, axis=1))
    return x * cos + partner * sin_signed


SMALL_W_OFF, SMALL_F_OFF, SMALL_KI_OFF = 0, HI, LANES - DI


def _proj_kernel(h_ref, c16_ref, s16_ref, c8_ref, s8_ref, wa_ref, wbq_ref, wbc_ref, wiq_ref, wc_ref,
                 wsm_ref, bf_ref, kvg_ref, wukv_ref,
                 qa_ref, ka_ref, vat_ref, qb_ref, kb_ref, vbt_ref, qi_ref, ki2_ref, sm_ref,
                 qc_ref, kc_ref, vct_ref, carry_ref, *, tiles_per_seq):
    i = pl.program_id(0)
    tm = h_ref.shape[0]
    hb = h_ref[...].astype(MXU_DTYPE)
    lane = lax.broadcasted_iota(jnp.int32, (1, LANES), 1)
    first16 = lane < (DB // 8)
    first8 = (lane % DI) < (DI // 8)
    c16, s16, c8, s8 = c16_ref[...], s16_ref[...], c8_ref[...], s8_ref[...]
    rope16 = lambda v: _rope_chunk(v, c16, s16, first16, DB // 8)
    rope8 = lambda v: _rope_chunk(v, c8, s8, first8, DI // 8)
    odt = qa_ref.dtype

    ya = _dot(hb, wa_ref[...])
    qa_ref[...] = (ya[:, :HA * DA] * (DA ** -0.5 * LOG2E)).astype(odt)
    for c in range(HA):
        ka_ref[:, 2 * c * DA:(2 * c + 1) * DA] = ya[:, (HA + c) * DA:(HA + c + 1) * DA].astype(odt)
    vat_ref[...] = ya[:, 2 * HA * DA:].T.astype(odt)

    yq = _dot(hb, wbq_ref[...])
    for c in range(HB):
        qb_ref[:, c * LANES:(c + 1) * LANES] = (
            rope16(yq[:, c * LANES:(c + 1) * LANES]) * (DB ** -0.5 * LOG2E)).astype(odt)
    lat = _dot(hb, wbc_ref[...])
    lat = lat * lax.rsqrt(jnp.mean(lat * lat, axis=-1, keepdims=True) + RMS_EPS) * kvg_ref[...]
    kv = _dot(lat.astype(MXU_DTYPE), wukv_ref[...])
    kb_ref[...] = rope16(kv[:, :DB]).astype(odt)
    vbt_ref[...] = kv[:, DB:].T.astype(odt)

    yi = _dot(hb, wiq_ref[...])
    for c in range(HI * DI // LANES):
        qi_ref[:, c * LANES:(c + 1) * LANES] = (
            rope8(yi[:, c * LANES:(c + 1) * LANES]) * (DI ** -0.5)).astype(odt)

    ysm = _dot(hb, wsm_ref[...])
    sm_ref[...] = ysm * (HI ** -0.5)
    ki = rope8(ysm)
    ki2_ref[...] = jnp.where(lane < SMALL_KI_OFF, pltpu.roll(ki, DI, axis=1), ki).astype(odt)

    z = ysm + bf_ref[...]
    logf = jnp.minimum(z, 0.0) - jnp.log1p(jnp.exp(-jnp.abs(z)))
    r = lax.broadcasted_iota(jnp.int32, (tm, tm), 0)
    cidx = lax.broadcasted_iota(jnp.int32, (tm, tm), 1)
    tri = jnp.where(r >= cidx, 1.0, 0.0).astype(jnp.float32)
    local = jnp.dot(tri, logf, preferred_element_type=jnp.float32, precision=lax.Precision.HIGHEST)

    @pl.when(i % tiles_per_seq == 0)
    def _():
        carry_ref[...] = jnp.zeros_like(carry_ref)

    cum = local + carry_ref[...]
    carry_ref[...] = cum[tm - 1:tm, :]
    for c in range(HA):
        bias = cum[:, SMALL_F_OFF + c:SMALL_F_OFF + c + 1] * (-LOG2E)
        hi = bias.astype(odt).astype(jnp.float32)
        mid = (bias - hi).astype(odt).astype(jnp.float32)
        lo = bias - hi - mid
        pieces = jnp.where(lane == 0, hi, jnp.where(lane == 1, mid, jnp.where(lane == 2, lo, 0.0)))
        ka_ref[:, (2 * c + 1) * DA:(2 * c + 2) * DA] = pieces.astype(odt)

    yc = _dot(hb, wc_ref[...])
    nqc = HC * 2 * DC
    for c in range(nqc // LANES):
        qc_ref[:, c * LANES:(c + 1) * LANES] = (
            rope8(yc[:, c * LANES:(c + 1) * LANES]) * (DC ** -0.5 * LOG2E)).astype(odt)
        kc_ref[:, c * LANES:(c + 1) * LANES] = rope8(
            yc[:, nqc + c * LANES:nqc + (c + 1) * LANES]).astype(odt)
    vct_ref[...] = yc[:, 2 * nqc:].T.astype(odt)


def _projections(h, tables, lw, seq, tm):
    t = h.shape[0]
    row = lambda n: pl.BlockSpec((tm, n), lambda i: (i, 0))
    col = lambda n: pl.BlockSpec((n, tm), lambda i: (0, i))
    outs = [(HA * DA, MXU_DTYPE, 0), (HA * 2 * DA, MXU_DTYPE, 0), (HA * DA, MXU_DTYPE, 1),
            (HB * DB, MXU_DTYPE, 0), (DB, MXU_DTYPE, 0), (DB, MXU_DTYPE, 1),
            (HI * DI, MXU_DTYPE, 0), (LANES, MXU_DTYPE, 0), (LANES, jnp.float32, 0),
            (HC * 2 * DC, MXU_DTYPE, 0), (HC * 2 * DC, MXU_DTYPE, 0), (HC * DCV, MXU_DTYPE, 1)]
    weights = [lw["wa"], lw["wbq"], lw["wbc"], lw["wiq"], lw["wc"], lw["wsm"], lw["bf"], lw["kvg"],
               lw["wukv"]]
    return pl.pallas_call(
        functools.partial(_proj_kernel, tiles_per_seq=seq // tm),
        grid=(t // tm,),
        in_specs=[row(D_MODEL)] + [row(LANES)] * 4 + [_const_spec(w.shape) for w in weights],
        out_specs=[col(n) if ax else row(n) for n, _, ax in outs],
        out_shape=[jax.ShapeDtypeStruct((n, t) if ax else (t, n), dt) for n, dt, ax in outs],
        scratch_shapes=[pltpu.VMEM((1, LANES), jnp.float32)],
        compiler_params=_params(("arbitrary",)), name="projections",
    )(h, *tables, *weights)


STRIP = 32
SUM_ROWS = 16


def _flash_scratch(chains, keys, width, tq):
    return [pltpu.VMEM((2, chains, keys, tq), jnp.float32),
            pltpu.VMEM((2, chains, SUBLANES, tq), jnp.float32),
            pltpu.VMEM((chains, keys, tq), MXU_DTYPE),
            pltpu.VMEM((chains, 1, tq), jnp.float32),
            pltpu.VMEM((chains, width + SUM_ROWS, tq), jnp.float32)]


def _chain_reset(scratch):
    m_ref, acc_ref = scratch[3], scratch[4]
    m_ref[...] = jnp.full(m_ref.shape, -jnp.inf, jnp.float32)
    acc_ref[...] = jnp.zeros(acc_ref.shape, jnp.float32)


def _fold(x, op):
    rows, tq = x.shape
    return op(x.reshape(rows // SUBLANES, SUBLANES, tq), axis=0)


def _store_logits(slot, c, s, scratch):
    s_ref, mp_ref = scratch[0], scratch[1]
    s_ref[slot, c] = s
    part = _fold(s[0:STRIP], jnp.max)
    for r0 in range(STRIP, s.shape[0], STRIP):
        part = jnp.maximum(part, _fold(s[r0:r0 + STRIP], jnp.max))
    mp_ref[slot, c] = part


def _absorb(slot, vt_of, scratch, diagonal=False):
    s_ref, mp_ref, p_ref, m_ref, acc_ref = scratch
    _, chains, keys, tq = s_ref.shape
    alphas = []
    for c in range(chains):
        def strip(r0):
            x = s_ref[slot, c, r0:r0 + STRIP, :]
            if diagonal:
                kk = r0 + lax.broadcasted_iota(jnp.int32, (STRIP, tq), 0)
                x = jnp.where(kk <= lax.broadcasted_iota(jnp.int32, (STRIP, tq), 1), x, -jnp.inf)
            return x

        if diagonal:
            m_part = _fold(strip(0), jnp.max)
            for r0 in range(STRIP, keys, STRIP):
                m_part = jnp.maximum(m_part, _fold(strip(r0), jnp.max))
        else:
            m_part = mp_ref[slot, c]
        m_old = m_ref[c]
        m_new = jnp.maximum(m_old, jnp.max(m_part, axis=0, keepdims=True))
        alphas.append(jnp.exp2(m_old - m_new))
        m_ref[c] = m_new
        for r0 in range(0, keys, STRIP):
            p_ref[c, r0:r0 + STRIP, :] = jnp.exp2(strip(r0) - m_new).astype(p_ref.dtype)
    ones = jnp.ones((SUM_ROWS, keys), p_ref.dtype)
    for c in range(chains):
        acc_ref[c] = alphas[c] * acc_ref[c] + _dot(jnp.concatenate([vt_of(c), ones], axis=0), p_ref[c])


def _chain_output(c, scratch):
    acc_ref = scratch[4]
    width = acc_ref.shape[1] - SUM_ROWS
    return (acc_ref[c, :width, :] / acc_ref[c, width:width + 1, :]).T


def _pipelined_chunks(last, issue, absorb):
    issue(0, 0)

    def pair(jj, _):
        j = 2 * jj
        issue(j + 1, 1)
        absorb(j, 0, False)
        issue(j + 2, 0)
        absorb(j + 1, 1, False)
        return 0

    lax.fori_loop(0, last // 2, pair, 0)
    odd = last % 2 == 1

    @pl.when(odd)
    def _():
        issue(last, 1)
        absorb(last - 1, 0, False)
        absorb(last, 1, True)

    @pl.when(jnp.logical_not(odd))
    def _():
        absorb(last, 0, True)


def _transposed(x):
    return x.astype(jnp.float32).T.astype(MXU_DTYPE)


FOX_HEADS = 4
DIFF_HEADS = 2


def _fox_kernel(q_ref, k_ref, vt_ref, o_ref, qt_ref, *scratch, heads):
    qi = pl.program_id(2)
    tq = q_ref.shape[1]
    _chain_reset(scratch)
    for g in range(heads):
        qt_ref[g, :DA, :] = _transposed(q_ref[0, :, g * DA:(g + 1) * DA])
        qt_ref[g, DA:, :] = jnp.ones((DA, tq), MXU_DTYPE)

    def issue(j, slot):
        start = pl.multiple_of(j * tq, tq)
        for g in range(heads):
            k = k_ref[0, pl.ds(start, tq), g * 2 * DA:(g + 1) * 2 * DA]
            _store_logits(slot, g, _dot(k, qt_ref[g]), scratch)

    def absorb(j, slot, diagonal):
        start = pl.multiple_of(j * tq, tq)
        _absorb(slot, lambda g: vt_ref[g * DA:(g + 1) * DA, pl.ds(start, tq)], scratch,
                diagonal=diagonal)

    _pipelined_chunks(qi, issue, absorb)
    for g in range(heads):
        o_ref[0, :, g * DA:(g + 1) * DA] = _chain_output(g, scratch).astype(o_ref.dtype)


def _fox_attention(qa, kaug, va_t, tq, heads):
    b, s, _ = qa.shape
    qspec = pl.BlockSpec((1, tq, heads * DA), lambda bi, h, i: (bi, i, h))
    kspec = pl.BlockSpec((1, s, heads * 2 * DA), lambda bi, h, i: (bi, 0, h),
                         pipeline_mode=pl.Buffered(1))
    vspec = pl.BlockSpec((heads * DA, s), lambda bi, h, i: (h, bi), pipeline_mode=pl.Buffered(1))
    return pl.pallas_call(
        functools.partial(_fox_kernel, heads=heads),
        grid=(b, HA // heads, s // tq), in_specs=[qspec, kspec, vspec], out_specs=qspec,
        out_shape=jax.ShapeDtypeStruct((b, s, HA * DA), MXU_DTYPE),
        scratch_shapes=[pltpu.VMEM((heads, 2 * DA, tq), MXU_DTYPE)] + _flash_scratch(heads, tq, DA, tq),
        compiler_params=_params(("parallel", "parallel", "arbitrary")), name="fox_attention",
    )(qa, kaug, va_t)


def _diff_kernel(q_ref, k_ref, vt_ref, lq_ref, g_ref, o_ref, qt_ref, *scratch, heads, lam_init):
    qi = pl.program_id(2)
    tq = q_ref.shape[1]
    _chain_reset(scratch)
    row = lax.broadcasted_iota(jnp.int32, (2 * DC, 1), 0)
    for g in range(heads):
        both = _transposed(q_ref[0, :, g * 2 * DC:(g + 1) * 2 * DC])
        zero = jnp.zeros_like(both)
        qt_ref[2 * g] = jnp.where(row < DC, both, zero)
        qt_ref[2 * g + 1] = jnp.where(row >= DC, both, zero)

    def issue(j, slot):
        start = pl.multiple_of(j * tq, tq)
        for c in range(2 * heads):
            g = c // 2
            k = k_ref[0, pl.ds(start, tq), g * 2 * DC:(g + 1) * 2 * DC]
            _store_logits(slot, c, _dot(k, qt_ref[c]), scratch)

    def absorb(j, slot, diagonal):
        start = pl.multiple_of(j * tq, tq)
        _absorb(slot, lambda c: vt_ref[(c // 2) * DCV:(c // 2 + 1) * DCV, pl.ds(start, tq)], scratch,
                diagonal=diagonal)

    _pipelined_chunks(qi, issue, absorb)
    lq = lq_ref[...]
    lam = (jnp.exp(jnp.sum(lq[0:1] * lq[1:2], axis=-1, keepdims=True))
           - jnp.exp(jnp.sum(lq[2:3] * lq[3:4], axis=-1, keepdims=True)) + lam_init)
    for g in range(heads):
        o = _chain_output(2 * g, scratch) - lam * _chain_output(2 * g + 1, scratch)
        o = o * lax.rsqrt(jnp.mean(o * o, axis=-1, keepdims=True) + RMS_EPS) * g_ref[...]
        o_ref[0, :, g * DCV:(g + 1) * DCV] = (o * (1.0 - lam_init)).astype(o_ref.dtype)


def _diff_attention(qc, kc, vc_t, lam_qk, norm_g, lam_init, tq, heads):
    b, s, _ = qc.shape
    qspec = pl.BlockSpec((1, tq, heads * DCV), lambda bi, h, i: (bi, i, h))
    kspec = pl.BlockSpec((1, s, heads * 2 * DC), lambda bi, h, i: (bi, 0, h),
                         pipeline_mode=pl.Buffered(1))
    vspec = pl.BlockSpec((heads * DCV, s), lambda bi, h, i: (h, bi), pipeline_mode=pl.Buffered(1))
    return pl.pallas_call(
        functools.partial(_diff_kernel, heads=heads, lam_init=lam_init),
        grid=(b, HC // heads, s // tq),
        in_specs=[qspec, kspec, vspec, _const_spec((4, DC)), _const_spec((1, DCV))],
        out_specs=qspec, out_shape=jax.ShapeDtypeStruct((b, s, HC * DCV), MXU_DTYPE),
        scratch_shapes=[pltpu.VMEM((2 * heads, 2 * DC, tq), MXU_DTYPE)]
        + _flash_scratch(2 * heads, tq, DCV, tq),
        compiler_params=_params(("parallel", "parallel", "arbitrary")), name="diff_attention",
    )(qc, kc, vc_t, lam_qk, norm_g.reshape(1, DCV))


COUNT_ROWS = 4 * SUBLANES
BISECT_STEPS_PER_CHECK = 4

def _dsa_kernel(qi_ref, w_ref, ki2_ref, qb_ref, kb_ref, vbt_ref, o_ref, key_ref, bias_ref, qt_ref,
                *scratch, top_k, tk):
    qt = pl.program_id(1)
    tq = qi_ref.shape[1]
    seq = key_ref.shape[0]
    nch = (qt * tq) // tk + 1
    q_pos = qt * tq + lax.broadcasted_iota(jnp.int32, (1, tq), 1)
    k_in_chunk = lax.broadcasted_iota(jnp.int32, (tk, 1), 0)
    row = lax.broadcasted_iota(jnp.int32, (2 * DI, 1), 0)

    w_t = w_ref[0].T
    q_heads = []
    for c in range(HI * DI // LANES):
        pair = _transposed(qi_ref[0, :, c * LANES:(c + 1) * LANES])
        zero = jnp.zeros_like(pair)
        q_heads += [jnp.where(row < DI, pair, zero), jnp.where(row >= DI, pair, zero)]

    def score_chunk(c, _):
        start = pl.multiple_of(c * tk, tk)
        ki2 = ki2_ref[0, pl.ds(start, tk), :]
        sc = jnp.zeros((tk, tq), jnp.float32)
        for h in range(HI):
            sc = sc + w_t[SMALL_W_OFF + h:SMALL_W_OFF + h + 1, :] * jnp.maximum(
                _dot(ki2, q_heads[h]), 0.0)
        sc = jnp.where(sc == 0.0, 0.0, sc)
        sc = jnp.where(start + k_in_chunk <= q_pos, sc, -jnp.inf)
        bits = pltpu.bitcast(sc, jnp.int32)
        key_ref[pl.ds(start, tk), :] = bits ^ ((bits >> 31) & INT32_MAX)
        return 0

    lax.fori_loop(0, nch, score_chunk, 0)

    def count(pred):
        def body(c, acc):
            start = pl.multiple_of(c * tk, tk)
            ones = jnp.where(pred(key_ref[pl.ds(start, tk), :], start), 1.0, 0.0)
            return acc + jnp.sum(ones.reshape(tk // COUNT_ROWS, COUNT_ROWS, tq), axis=0)
        acc = lax.fori_loop(0, nch, body, jnp.zeros((COUNT_ROWS, tq), jnp.float32))
        return jnp.sum(acc, axis=0, keepdims=True)

    k_sel = jnp.minimum(top_k, q_pos + 1).astype(jnp.float32)
    total = (nch * tk).astype(jnp.float32)

    n_pos = count(lambda x, start: x >= 1)
    n_nonneg = count(lambda x, start: x >= 0)
    pos, neg = n_pos >= k_sel, n_nonneg < k_sel
    lo = jnp.where(pos, 1, jnp.where(neg, INT32_MIN, 0)).astype(jnp.int32)
    hi = jnp.where(pos, INT32_MAX, jnp.where(neg, 0, 1)).astype(jnp.int32)
    n_lo = jnp.where(pos, n_pos, jnp.where(neg, total, n_nonneg))

    def unsettled(lo, hi, n_lo):
        return jnp.max(jnp.where((n_lo != k_sel) & (hi != lo + 1), 1.0, 0.0)) > 0.0

    def bisect(carry):
        _, lo, hi, n_lo = carry
        for _ in range(BISECT_STEPS_PER_CHECK):
            open_ = (n_lo != k_sel) & (hi != lo + 1)
            mid = (lo & hi) + ((lo ^ hi) >> 1)
            n_mid = count(lambda x, start: x >= mid)
            up = open_ & (n_mid >= k_sel)
            down = open_ & (n_mid < k_sel)
            lo, hi, n_lo = (jnp.where(up, mid, lo), jnp.where(down, mid, hi),
                            jnp.where(up, n_mid, n_lo))
        return unsettled(lo, hi, n_lo), lo, hi, n_lo

    _, thr, _, n_ge = lax.while_loop(lambda carry: carry[0], bisect,
                                     (unsettled(lo, hi, n_lo), lo, hi, n_lo))

    def tie_cutoff():
        need = k_sel - count(lambda x, start: x > thr)

        def step(_, lohi):
            lo, hi = lohi
            mid = (lo + hi) >> 1
            got = count(lambda x, start: (x == thr) & (start + k_in_chunk <= mid))
            ok = got >= need
            return jnp.where(ok, lo, mid), jnp.where(ok, mid, hi)
        steps = max(1, math.ceil(math.log2(seq))) + 1
        _, hi = lax.fori_loop(0, steps, step, (jnp.full((1, tq), -1, jnp.int32),
                                               jnp.full((1, tq), seq - 1, jnp.int32)))
        return hi

    any_tie = jnp.max(jnp.where(n_ge != k_sel, 1.0, 0.0)) > 0.0
    cutoff = lax.cond(any_tie, tie_cutoff, lambda: jnp.full((1, tq), seq - 1, jnp.int32))

    def bias_chunk(c, _):
        start = pl.multiple_of(c * tk, tk)
        x = key_ref[pl.ds(start, tk), :]
        sel = (x > thr) | ((x == thr) & (start + k_in_chunk <= cutoff))
        bias_ref[pl.ds(start, tk), :] = jnp.where(sel, 0.0, MASKED)
        return 0

    lax.fori_loop(0, nch, bias_chunk, 0)

    _chain_reset(scratch)
    for h in range(HB):
        qt_ref[h] = _transposed(qb_ref[0, :, h * DB:(h + 1) * DB])

    def issue(c, slot):
        start = pl.multiple_of(c * tk, tk)
        k = kb_ref[0, pl.ds(start, tk), :]
        bias = bias_ref[pl.ds(start, tk), :]
        for h in range(HB):
            _store_logits(slot, h, _dot(k, qt_ref[h]) + bias, scratch)

    def absorb(c, slot, _):
        start = pl.multiple_of(c * tk, tk)
        _absorb(slot, lambda h: vbt_ref[:, pl.ds(start, tk)], scratch)

    _pipelined_chunks(nch - 1, issue, absorb)
    for h in range(HB):
        o_ref[0, :, h * DB:(h + 1) * DB] = _chain_output(h, scratch).astype(o_ref.dtype)


def _dsa_attention(qi, sm, ki2, qb, kb, vb_t, top_k, tq, tk):
    b, s, _ = qb.shape
    qrow = lambda n: pl.BlockSpec((1, tq, n), lambda bi, i: (bi, i, 0))
    full = lambda n: pl.BlockSpec((1, s, n), lambda bi, i: (bi, 0, 0), pipeline_mode=pl.Buffered(1))
    vspec = pl.BlockSpec((DB, s), lambda bi, i: (0, bi), pipeline_mode=pl.Buffered(1))
    return pl.pallas_call(
        functools.partial(_dsa_kernel, top_k=top_k, tk=tk),
        grid=(b, s // tq),
        in_specs=[qrow(HI * DI), qrow(LANES), full(LANES), qrow(HB * DB), full(DB), vspec],
        out_specs=qrow(HB * DB), out_shape=jax.ShapeDtypeStruct((b, s, HB * DB), MXU_DTYPE),
        scratch_shapes=[pltpu.VMEM((s, tq), jnp.int32), pltpu.VMEM((s, tq), jnp.float32),
                        pltpu.VMEM((HB, DB, tq), MXU_DTYPE)] + _flash_scratch(HB, tk, DB, tq),
        compiler_params=_params(("parallel", "arbitrary")), name="dsa_attention",
    )(qi, sm, ki2, qb, kb, vb_t)


def _merge_kernel(h_ref, oa_ref, ob_ref, oc_ref, wg_ref, wa_ref, wb_ref, wc_ref, wo_ref, g_ref, b_ref,
                  out_ref):
    h = h_ref[...]
    gates = 1.0 / (1.0 + jnp.exp(-_dot(h.astype(MXU_DTYPE), wg_ref[...])))
    merged = (gates[:, :D_MODEL] * _dot(oa_ref[...], wa_ref[...])
              + gates[:, D_MODEL:2 * D_MODEL] * _dot(ob_ref[...], wb_ref[...])
              + gates[:, 2 * D_MODEL:] * _dot(oc_ref[...], wc_ref[...]))
    mix = _dot(merged.astype(MXU_DTYPE), wo_ref[...])
    out_ref[...] = _layer_norm_rows(ALPHA * h + mix, g_ref[...], b_ref[...])


def _merge(h, oa, ob, oc, lw, tm):
    t = h.shape[0]
    row = lambda n: pl.BlockSpec((tm, n), lambda i: (i, 0))
    weights = [lw["wg"], lw["wbr_a"], lw["wbr_b"], lw["wbr_c"], lw["wo"], lw["ln1_g"], lw["ln1_b"]]
    return pl.pallas_call(
        _merge_kernel, grid=(t // tm,),
        in_specs=[row(D_MODEL), row(HA * DA), row(HB * DB), row(HC * DCV)]
        + [_const_spec(w.shape) for w in weights],
        out_specs=row(D_MODEL), out_shape=jax.ShapeDtypeStruct((t, D_MODEL), jnp.float32),
        compiler_params=_params(("parallel",)), name="merge",
    )(h, oa, ob, oc, *weights)


FF_CHUNK = D_FF // 2
CONV_HALO = SUBLANES


def _ffn_kernel(h_ref, wup_ref, cw_ref, cb_ref, wdn_ref, g_ref, b_ref, out_ref, gate_ref,
                *, tiles_per_seq):
    i = pl.program_id(0)
    tm = h_ref.shape[0]
    h = h_ref[...]
    hb = h.astype(MXU_DTYPE)

    @pl.when(i % tiles_per_seq == 0)
    def _():
        gate_ref[0:CONV_HALO, :] = jnp.zeros((CONV_HALO, D_FF), jnp.float32)

    ffn = jnp.zeros((tm, D_MODEL), jnp.float32)
    for f0 in range(0, D_FF, FF_CHUNK):
        f1 = f0 + FF_CHUNK
        gate_ref[CONV_HALO:, f0:f1] = _dot(hb, wup_ref[:, f0:f1])
        val = _dot(hb, wup_ref[:, D_FF + f0:D_FF + f1])
        conv = cb_ref[:, f0:f1]
        for j in range(CONV_W):
            off = CONV_HALO - (CONV_W - 1) + j
            conv = conv + cw_ref[j:j + 1, f0:f1] * gate_ref[off:off + tm, f0:f1]
        act = conv / (1.0 + jnp.exp(-conv)) * val
        ffn = ffn + _dot(act.astype(MXU_DTYPE), wdn_ref[f0:f1, :])
    gate_ref[0:CONV_HALO, :] = gate_ref[tm:tm + CONV_HALO, :]
    out_ref[...] = _layer_norm_rows(ALPHA * h + ffn, g_ref[...], b_ref[...])


def _conv_gated_mlp(h, lw, seq, tm):
    t = h.shape[0]
    row = pl.BlockSpec((tm, D_MODEL), lambda i: (i, 0))
    weights = [lw["wup"], lw["conv_w"], lw["conv_b"], lw["wdn"], lw["ln2_g"], lw["ln2_b"]]
    return pl.pallas_call(
        functools.partial(_ffn_kernel, tiles_per_seq=seq // tm), grid=(t // tm,),
        in_specs=[row] + [_const_spec(w.shape) for w in weights],
        out_specs=row, out_shape=jax.ShapeDtypeStruct((t, D_MODEL), jnp.float32),
        scratch_shapes=[pltpu.VMEM((tm + CONV_HALO, D_FF), jnp.float32)],
        compiler_params=_params(("arbitrary",)), name="conv_gated_mlp",
    )(h, *weights)


def _layer_weights(l, w_in, b_f, kv_norm_g, w_ukv, w_gate, w_br_a, w_br_b, w_br_c, w_out, ln1_g, ln1_b,
                   w_up, conv_w, conv_b, w_down, ln2_g, ln2_b):
    widths = (HA * DA, HA * DA, HA * DA, HA, HB * DB, KV_LATENT, HI * DI, DI, HI,
              HC * 2 * DC, HC * 2 * DC, HC * DCV)
    offs = [0]
    for n in widths:
        offs.append(offs[-1] + n)
    col = lambda k: w_in[l][:, offs[k]:offs[k + 1]]
    aq, ak, av, af, bq, bc, biq, bik, biw, cq, ck, cv = (col(k) for k in range(len(widths)))
    pad = jnp.zeros((D_MODEL, LANES - HI - HA - DI), w_in.dtype)
    mx = lambda a: a.astype(MXU_DTYPE)
    vec = lambda a: a.reshape(1, -1).astype(jnp.float32)
    bf = jnp.zeros((1, LANES), jnp.float32).at[0, SMALL_F_OFF:SMALL_F_OFF + HA].set(b_f[l])
    return dict(
        wa=mx(jnp.concatenate([aq, ak, av], axis=1)), wbq=mx(bq), wbc=mx(bc), wiq=mx(biq),
        wc=mx(jnp.concatenate([cq, ck, cv], axis=1)),
        wsm=mx(jnp.concatenate([biw, af, pad, bik], axis=1)), bf=bf, kvg=vec(kv_norm_g[l]),
        wukv=mx(w_ukv[l]), wg=mx(w_gate[l]), wbr_a=mx(w_br_a[l]), wbr_b=mx(w_br_b[l]),
        wbr_c=mx(w_br_c[l]), wo=mx(w_out[l]), ln1_g=vec(ln1_g[l]), ln1_b=vec(ln1_b[l]),
        wup=mx(w_up[l]), conv_w=conv_w[l].astype(jnp.float32), conv_b=vec(conv_b[l]),
        wdn=mx(w_down[l]), ln2_g=vec(ln2_g[l]), ln2_b=vec(ln2_b[l]))


def _tile(n, want):
    t = min(n, want)
    assert n % t == 0, (n, t)
    return t


def kernel(x, positions, ln_in_g, ln_in_b, w_in, b_f, kv_norm_g, w_ukv, lam_qk, diff_norm_g, w_gate,
           w_br_a, w_br_b, w_br_c, w_out, ln1_g, ln1_b, w_up, conv_w, conv_b, w_down, ln2_g, ln2_b):
    b, s, d = x.shape
    assert d == D_MODEL and s % LANES == 0
    t = b * s
    depth = w_in.shape[0]
    top_k = min(TOPK_MAX, s // 4)
    tm = _tile(s, 512)
    tq = _tile(s, 256)
    tk_dsa = _tile(s, 512)

    h = _input_layer_norm(x.reshape(t, d), ln_in_g, ln_in_b, tm)
    pos_lanes = jnp.broadcast_to(positions.reshape(t, 1).astype(jnp.float32), (t, LANES))
    tables = _rope_tables(pos_lanes, tm)

    for l in range(depth):
        lam_init = 0.8 - 0.6 * math.exp(-0.3 * l)
        lw = _layer_weights(l, w_in, b_f, kv_norm_g, w_ukv, w_gate, w_br_a, w_br_b, w_br_c, w_out,
                            ln1_g, ln1_b, w_up, conv_w, conv_b, w_down, ln2_g, ln2_b)
        (qa, ka, va_t, qb, kb, vb_t, qi, ki2, sm, qc, kc, vc_t) = _projections(h, tables, lw, s, tm)
        seq3 = lambda a: a.reshape(b, s, a.shape[-1])
        o_a = _fox_attention(seq3(qa), seq3(ka), va_t, tq, FOX_HEADS)
        o_b = _dsa_attention(seq3(qi), seq3(sm), seq3(ki2), seq3(qb), seq3(kb), vb_t,
                             top_k, tq, tk_dsa)
        o_c = _diff_attention(seq3(qc), seq3(kc), vc_t, lam_qk[l].astype(jnp.float32),
                              diff_norm_g[l].astype(jnp.float32), lam_init, tq, DIFF_HEADS)
        h = _merge(h, o_a.reshape(t, -1), o_b.reshape(t, -1), o_c.reshape(t, -1), lw, tm)
        h = _conv_gated_mlp(h, lw, s, tm)
    return h.reshape(b, s, d)
```

```python
import functools
import math

import jax
import jax.numpy as jnp
from jax import lax
from jax.experimental import pallas as pl
from jax.experimental.pallas import tpu as pltpu

D_MODEL = 1024
ROPE_THETA = 500000.0
HA, DA = 4, 128
HB, DB = 4, 128
KV_LATENT = 256
HI, DI = 8, 64
TOPK_MAX = 256
HC, DC, DCV = 4, 64, 128
D_FF = 2816
CONV_W = 3
MODEL_DEPTH = 4
ALPHA = (2 * MODEL_DEPTH) ** 0.25
LN_EPS = 1e-5
RMS_EPS = 1e-6

LANES = 128
SUBLANES = 8
VMEM_LIMIT_BYTES = 56 * 1024 * 1024
MXU_DTYPE = jnp.bfloat16
MASKED = -1e30
LOG2E = math.log2(math.e)
INT32_MIN = -2 ** 31
INT32_MAX = 2 ** 31 - 1


def _dot(a, b):
    return jnp.dot(a, b, preferred_element_type=jnp.float32)


def _layer_norm_rows(x, g, b):
    mu = jnp.mean(x, axis=-1, keepdims=True)
    xc = x - mu
    var = jnp.mean(xc * xc, axis=-1, keepdims=True)
    return xc * lax.rsqrt(var + LN_EPS) * g + b


def _params(semantics):
    return pltpu.CompilerParams(dimension_semantics=semantics, vmem_limit_bytes=VMEM_LIMIT_BYTES)


def _const_spec(shape):
    nd = len(shape)
    return pl.BlockSpec(shape, lambda *_: (0,) * nd, pipeline_mode=pl.Buffered(1))


def _ln_kernel(x_ref, g_ref, b_ref, o_ref):
    o_ref[...] = _layer_norm_rows(x_ref[...], g_ref[...], b_ref[...])


def _input_layer_norm(x2d, g, b, tm):
    t, d = x2d.shape
    row = pl.BlockSpec((tm, d), lambda i: (i, 0))
    return pl.pallas_call(
        _ln_kernel, grid=(t // tm,), in_specs=[row, _const_spec((1, d)), _const_spec((1, d))],
        out_specs=row, out_shape=jax.ShapeDtypeStruct((t, d), jnp.float32),
        compiler_params=_params(("parallel",)), name="input_ln",
    )(x2d, g.reshape(1, d), b.reshape(1, d))


def _rope_table_kernel(pos_ref, f16_ref, g16_ref, f8_ref, g8_ref, c16_ref, s16_ref, c8_ref, s8_ref):
    p = pos_ref[...]
    a16 = p * f16_ref[...]
    c16_ref[...] = jnp.cos(a16)
    s16_ref[...] = jnp.sin(a16) * g16_ref[...]
    a8 = p * f8_ref[...]
    c8_ref[...] = jnp.cos(a8)
    s8_ref[...] = jnp.sin(a8) * g8_ref[...]


def _lane_freq_sign(head_dim):
    rot = head_dim // 4
    half = rot // 2
    inv_freq = ROPE_THETA ** (-jnp.arange(half, dtype=jnp.float32) / half)
    d = jnp.arange(LANES) % head_dim
    freq = jnp.where(d < rot, inv_freq[d % half], 0.0).astype(jnp.float32)
    sign = jnp.where(d < half, -1.0, jnp.where(d < rot, 1.0, 0.0)).astype(jnp.float32)
    return freq.reshape(1, LANES), sign.reshape(1, LANES)


def _rope_tables(pos_lanes, tm):
    t = pos_lanes.shape[0]
    f16, g16 = _lane_freq_sign(DB)
    f8, g8 = _lane_freq_sign(DI)
    row = pl.BlockSpec((tm, LANES), lambda i: (i, 0))
    vec = _const_spec((1, LANES))
    tab = jax.ShapeDtypeStruct((t, LANES), jnp.float32)
    return pl.pallas_call(
        _rope_table_kernel, grid=(t // tm,), in_specs=[row, vec, vec, vec, vec],
        out_specs=[row] * 4, out_shape=[tab] * 4,
        compiler_params=_params(("parallel",)), name="rope_tables",
    )(pos_lanes, f16, g16, f8, g8)


def _rope_chunk(x, cos, sin_signed, first_half, half):
    partner = jnp.where(first_half, pltpu.roll(x, LANES - half, axis=1), pltpu.roll(x, half, axis=1))
    return x * cos + partner * sin_signed


SMALL_W_OFF, SMALL_F_OFF, SMALL_KI_OFF = 0, HI, LANES - DI


def _proj_kernel(h_ref, c16_ref, s16_ref, c8_ref, s8_ref, wa_ref, wbq_ref, wbc_ref, wiq_ref, wc_ref,
                 wsm_ref, bf_ref, kvg_ref, wukv_ref,
                 qa_ref, ka_ref, vat_ref, qb_ref, kb_ref, vbt_ref, qi_ref, ki2_ref, sm_ref,
                 qc_ref, kc_ref, vct_ref, carry_ref, *, tiles_per_seq):
    i = pl.program_id(0)
    tm = h_ref.shape[0]
    hb = h_ref[...].astype(MXU_DTYPE)
    lane = lax.broadcasted_iota(jnp.int32, (1, LANES), 1)
    first16 = lane < (DB // 8)
    first8 = (lane % DI) < (DI // 8)
    c16, s16, c8, s8 = c16_ref[...], s16_ref[...], c8_ref[...], s8_ref[...]
    rope16 = lambda v: _rope_chunk(v, c16, s16, first16, DB // 8)
    rope8 = lambda v: _rope_chunk(v, c8, s8, first8, DI // 8)
    odt = qa_ref.dtype

    ya = _dot(hb, wa_ref[...])
    qa_ref[...] = (ya[:, :HA * DA] * (DA ** -0.5 * LOG2E)).astype(odt)
    for c in range(HA):
        ka_ref[:, 2 * c * DA:(2 * c + 1) * DA] = ya[:, (HA + c) * DA:(HA + c + 1) * DA].astype(odt)
    vat_ref[...] = ya[:, 2 * HA * DA:].T.astype(odt)

    yq = _dot(hb, wbq_ref[...])
    for c in range(HB):
        qb_ref[:, c * LANES:(c + 1) * LANES] = (
            rope16(yq[:, c * LANES:(c + 1) * LANES]) * (DB ** -0.5 * LOG2E)).astype(odt)
    lat = _dot(hb, wbc_ref[...])
    lat = lat * lax.rsqrt(jnp.mean(lat * lat, axis=-1, keepdims=True) + RMS_EPS) * kvg_ref[...]
    kv = _dot(lat.astype(MXU_DTYPE), wukv_ref[...])
    kb_ref[...] = rope16(kv[:, :DB]).astype(odt)
    vbt_ref[...] = kv[:, DB:].T.astype(odt)

    yi = _dot(hb, wiq_ref[...])
    for c in range(HI * DI // LANES):
        qi_ref[:, c * LANES:(c + 1) * LANES] = (
            rope8(yi[:, c * LANES:(c + 1) * LANES]) * (DI ** -0.5)).astype(odt)

    ysm = _dot(hb, wsm_ref[...])
    sm_ref[...] = ysm * (HI ** -0.5)
    ki = rope8(ysm)
    ki2_ref[...] = jnp.where(lane < SMALL_KI_OFF, pltpu.roll(ki, DI, axis=1), ki).astype(odt)

    z = ysm + bf_ref[...]
    logf = jnp.minimum(z, 0.0) - jnp.log1p(jnp.exp(-jnp.abs(z)))
    r = lax.broadcasted_iota(jnp.int32, (tm, tm), 0)
    cidx = lax.broadcasted_iota(jnp.int32, (tm, tm), 1)
    tri = jnp.where(r >= cidx, 1.0, 0.0).astype(jnp.float32)
    local = jnp.dot(tri, logf, preferred_element_type=jnp.float32, precision=lax.Precision.HIGHEST)

    @pl.when(i % tiles_per_seq == 0)
    def _():
        carry_ref[...] = jnp.zeros_like(carry_ref)

    cum = local + carry_ref[...]
    carry_ref[...] = cum[tm - 1:tm, :]
    for c in range(HA):
        bias = cum[:, SMALL_F_OFF + c:SMALL_F_OFF + c + 1] * (-LOG2E)
        hi = bias.astype(odt).astype(jnp.float32)
        mid = (bias - hi).astype(odt).astype(jnp.float32)
        lo = bias - hi - mid
        pieces = jnp.where(lane == 0, hi, jnp.where(lane == 1, mid, jnp.where(lane == 2, lo, 0.0)))
        ka_ref[:, (2 * c + 1) * DA:(2 * c + 2) * DA] = pieces.astype(odt)

    yc = _dot(hb, wc_ref[...])
    nqc = HC * 2 * DC
    for c in range(nqc // LANES):
        qc_ref[:, c * LANES:(c + 1) * LANES] = (
            rope8(yc[:, c * LANES:(c + 1) * LANES]) * (DC ** -0.5 * LOG2E)).astype(odt)
        kc_ref[:, c * LANES:(c + 1) * LANES] = rope8(
            yc[:, nqc + c * LANES:nqc + (c + 1) * LANES]).astype(odt)
    vct_ref[...] = yc[:, 2 * nqc:].T.astype(odt)


def _projections(h, tables, lw, seq, tm):
    t = h.shape[0]
    row = lambda n: pl.BlockSpec((tm, n), lambda i: (i, 0))
    col = lambda n: pl.BlockSpec((n, tm), lambda i: (0, i))
    outs = [(HA * DA, MXU_DTYPE, 0), (HA * 2 * DA, MXU_DTYPE, 0), (HA * DA, MXU_DTYPE, 1),
            (HB * DB, MXU_DTYPE, 0), (DB, MXU_DTYPE, 0), (DB, MXU_DTYPE, 1),
            (HI * DI, MXU_DTYPE, 0), (LANES, MXU_DTYPE, 0), (LANES, jnp.float32, 0),
            (HC * 2 * DC, MXU_DTYPE, 0), (HC * 2 * DC, MXU_DTYPE, 0), (HC * DCV, MXU_DTYPE, 1)]
    weights = [lw["wa"], lw["wbq"], lw["wbc"], lw["wiq"], lw["wc"], lw["wsm"], lw["bf"], lw["kvg"],
               lw["wukv"]]
    return pl.pallas_call(
        functools.partial(_proj_kernel, tiles_per_seq=seq // tm),
        grid=(t // tm,),
        in_specs=[row(D_MODEL)] + [row(LANES)] * 4 + [_const_spec(w.shape) for w in weights],
        out_specs=[col(n) if ax else row(n) for n, _, ax in outs],
        out_shape=[jax.ShapeDtypeStruct((n, t) if ax else (t, n), dt) for n, dt, ax in outs],
        scratch_shapes=[pltpu.VMEM((1, LANES), jnp.float32)],
        compiler_params=_params(("arbitrary",)), name="projections",
    )(h, *tables, *weights)


STRIP = 32
SUM_ROWS = 16


def _flash_scratch(chains, keys, width, tq):
    return [pltpu.VMEM((2, chains, keys, tq), jnp.float32),
            pltpu.VMEM((2, chains, SUBLANES, tq), jnp.float32),
            pltpu.VMEM((chains, keys, tq), MXU_DTYPE),
            pltpu.VMEM((chains, 1, tq), jnp.float32),
            pltpu.VMEM((chains, width + SUM_ROWS, tq), jnp.float32)]


def _chain_reset(scratch):
    m_ref, acc_ref = scratch[3], scratch[4]
    m_ref[...] = jnp.full(m_ref.shape, -jnp.inf, jnp.float32)
    acc_ref[...] = jnp.zeros(acc_ref.shape, jnp.float32)


def _fold(x, op):
    rows, tq = x.shape
    return op(x.reshape(rows // SUBLANES, SUBLANES, tq), axis=0)


def _store_logits(slot, c, s, scratch):
    s_ref, mp_ref = scratch[0], scratch[1]
    s_ref[slot, c] = s
    part = _fold(s[0:STRIP], jnp.max)
    for r0 in range(STRIP, s.shape[0], STRIP):
        part = jnp.maximum(part, _fold(s[r0:r0 + STRIP], jnp.max))
    mp_ref[slot, c] = part


def _absorb(slot, vt_of, scratch, diagonal=False):
    s_ref, mp_ref, p_ref, m_ref, acc_ref = scratch
    _, chains, keys, tq = s_ref.shape
    alphas = []
    for c in range(chains):
        def strip(r0):
            x = s_ref[slot, c, r0:r0 + STRIP, :]
            if diagonal:
                kk = r0 + lax.broadcasted_iota(jnp.int32, (STRIP, tq), 0)
                x = jnp.where(kk <= lax.broadcasted_iota(jnp.int32, (STRIP, tq), 1), x, -jnp.inf)
            return x

        if diagonal:
            m_part = _fold(strip(0), jnp.max)
            for r0 in range(STRIP, keys, STRIP):
                m_part = jnp.maximum(m_part, _fold(strip(r0), jnp.max))
        else:
            m_part = mp_ref[slot, c]
        m_old = m_ref[c]
        m_new = jnp.maximum(m_old, jnp.max(m_part, axis=0, keepdims=True))
        alphas.append(jnp.exp2(m_old - m_new))
        m_ref[c] = m_new
        for r0 in range(0, keys, STRIP):
            p_ref[c, r0:r0 + STRIP, :] = jnp.exp2(strip(r0) - m_new).astype(p_ref.dtype)
    ones = jnp.ones((SUM_ROWS, keys), p_ref.dtype)
    for c in range(chains):
        acc_ref[c] = alphas[c] * acc_ref[c] + _dot(jnp.concatenate([vt_of(c), ones], axis=0), p_ref[c])


def _chain_output(c, scratch):
    acc_ref = scratch[4]
    width = acc_ref.shape[1] - SUM_ROWS
    return (acc_ref[c, :width, :] / acc_ref[c, width:width + 1, :]).T


def _pipelined_chunks(last, issue, absorb):
    issue(last, 0)
    issue(0, 1)
    absorb(last, 0, True)

    def pair(jj, _):
        j = 2 * jj
        issue(j + 1, 0)
        absorb(j, 1, False)
        issue(jnp.minimum(j + 2, last), 1)
        absorb(j + 1, 0, False)
        return 0

    lax.fori_loop(0, last // 2, pair, 0)

    @pl.when(last % 2 == 1)
    def _():
        absorb(last - 1, 1, False)


def _transposed(x):
    return x.astype(jnp.float32).T.astype(MXU_DTYPE)


FOX_HEADS = 4
DIFF_HEADS = 2


def _fox_kernel(q_ref, k_ref, vt_ref, o_ref, qt_ref, *scratch, heads):
    qi = pl.program_id(2)
    tq = q_ref.shape[1]
    _chain_reset(scratch)
    for g in range(heads):
        qt_ref[g, :DA, :] = _transposed(q_ref[0, :, g * DA:(g + 1) * DA])
        qt_ref[g, DA:, :] = jnp.ones((DA, tq), MXU_DTYPE)

    def issue(j, slot):
        start = pl.multiple_of(j * tq, tq)
        for g in range(heads):
            k = k_ref[0, pl.ds(start, tq), g * 2 * DA:(g + 1) * 2 * DA]
            _store_logits(slot, g, _dot(k, qt_ref[g]), scratch)

    def absorb(j, slot, diagonal):
        start = pl.multiple_of(j * tq, tq)
        _absorb(slot, lambda g: vt_ref[g * DA:(g + 1) * DA, pl.ds(start, tq)], scratch,
                diagonal=diagonal)

    _pipelined_chunks(qi, issue, absorb)
    for g in range(heads):
        o_ref[0, :, g * DA:(g + 1) * DA] = _chain_output(g, scratch).astype(o_ref.dtype)


def _fox_attention(qa, kaug, va_t, tq, heads):
    b, s, _ = qa.shape
    qspec = pl.BlockSpec((1, tq, heads * DA), lambda bi, h, i: (bi, i, h))
    kspec = pl.BlockSpec((1, s, heads * 2 * DA), lambda bi, h, i: (bi, 0, h),
                         pipeline_mode=pl.Buffered(1))
    vspec = pl.BlockSpec((heads * DA, s), lambda bi, h, i: (h, bi), pipeline_mode=pl.Buffered(1))
    return pl.pallas_call(
        functools.partial(_fox_kernel, heads=heads),
        grid=(b, HA // heads, s // tq), in_specs=[qspec, kspec, vspec], out_specs=qspec,
        out_shape=jax.ShapeDtypeStruct((b, s, HA * DA), MXU_DTYPE),
        scratch_shapes=[pltpu.VMEM((heads, 2 * DA, tq), MXU_DTYPE)] + _flash_scratch(heads, tq, DA, tq),
        compiler_params=_params(("parallel", "parallel", "arbitrary")), name="fox_attention",
    )(qa, kaug, va_t)


def _diff_kernel(q_ref, k_ref, vt_ref, lq_ref, g_ref, o_ref, qt_ref, *scratch, heads, lam_init):
    qi = pl.program_id(2)
    tq = q_ref.shape[1]
    _chain_reset(scratch)
    row = lax.broadcasted_iota(jnp.int32, (2 * DC, 1), 0)
    for g in range(heads):
        both = _transposed(q_ref[0, :, g * 2 * DC:(g + 1) * 2 * DC])
        zero = jnp.zeros_like(both)
        qt_ref[2 * g] = jnp.where(row < DC, both, zero)
        qt_ref[2 * g + 1] = jnp.where(row >= DC, both, zero)

    def issue(j, slot):
        start = pl.multiple_of(j * tq, tq)
        for c in range(2 * heads):
            g = c // 2
            k = k_ref[0, pl.ds(start, tq), g * 2 * DC:(g + 1) * 2 * DC]
            _store_logits(slot, c, _dot(k, qt_ref[c]), scratch)

    def absorb(j, slot, diagonal):
        start = pl.multiple_of(j * tq, tq)
        _absorb(slot, lambda c: vt_ref[(c // 2) * DCV:(c // 2 + 1) * DCV, pl.ds(start, tq)], scratch,
                diagonal=diagonal)

    _pipelined_chunks(qi, issue, absorb)
    lq = lq_ref[...]
    lam = (jnp.exp(jnp.sum(lq[0:1] * lq[1:2], axis=-1, keepdims=True))
           - jnp.exp(jnp.sum(lq[2:3] * lq[3:4], axis=-1, keepdims=True)) + lam_init)
    for g in range(heads):
        o = _chain_output(2 * g, scratch) - lam * _chain_output(2 * g + 1, scratch)
        o = o * lax.rsqrt(jnp.mean(o * o, axis=-1, keepdims=True) + RMS_EPS) * g_ref[...]
        o_ref[0, :, g * DCV:(g + 1) * DCV] = (o * (1.0 - lam_init)).astype(o_ref.dtype)


def _diff_attention(qc, kc, vc_t, lam_qk, norm_g, lam_init, tq, heads):
    b, s, _ = qc.shape
    qspec = pl.BlockSpec((1, tq, heads * DCV), lambda bi, h, i: (bi, i, h))
    kspec = pl.BlockSpec((1, s, heads * 2 * DC), lambda bi, h, i: (bi, 0, h),
                         pipeline_mode=pl.Buffered(1))
    vspec = pl.BlockSpec((heads * DCV, s), lambda bi, h, i: (h, bi), pipeline_mode=pl.Buffered(1))
    return pl.pallas_call(
        functools.partial(_diff_kernel, heads=heads, lam_init=lam_init),
        grid=(b, HC // heads, s // tq),
        in_specs=[qspec, kspec, vspec, _const_spec((4, DC)), _const_spec((1, DCV))],
        out_specs=qspec, out_shape=jax.ShapeDtypeStruct((b, s, HC * DCV), MXU_DTYPE),
        scratch_shapes=[pltpu.VMEM((2 * heads, 2 * DC, tq), MXU_DTYPE)]
        + _flash_scratch(2 * heads, tq, DCV, tq),
        compiler_params=_params(("parallel", "parallel", "arbitrary")), name="diff_attention",
    )(qc, kc, vc_t, lam_qk, norm_g.reshape(1, DCV))


COUNT_ROWS = 4 * SUBLANES
BISECT_STEPS_PER_CHECK = 4
COARSE_DTYPE = jnp.bfloat16
TOP16 = -(1 << 16)
KEY_NEG_INF = (-(1 << 23)) ^ INT32_MAX
KEY_POS_INF = 0x7F800000


def _dsa_kernel(qi_ref, w_ref, ki2_ref, qb_ref, kb_ref, vbt_ref, o_ref, key_ref, coarse_ref, bias_ref,
                qt_ref, *scratch, top_k, tk):
    qt = pl.program_id(1)
    tq = qi_ref.shape[1]
    seq = key_ref.shape[0]
    nch = (qt * tq) // tk + 1
    q_pos = qt * tq + lax.broadcasted_iota(jnp.int32, (1, tq), 1)
    k_in_chunk = lax.broadcasted_iota(jnp.int32, (tk, 1), 0)
    row = lax.broadcasted_iota(jnp.int32, (2 * DI, 1), 0)

    w_t = w_ref[0].T
    q_heads = []
    for c in range(HI * DI // LANES):
        pair = _transposed(qi_ref[0, :, c * LANES:(c + 1) * LANES])
        zero = jnp.zeros_like(pair)
        q_heads += [jnp.where(row < DI, pair, zero), jnp.where(row >= DI, pair, zero)]

    def score_chunk(c, causal_edge):
        start = pl.multiple_of(c * tk, tk)
        ki2 = ki2_ref[0, pl.ds(start, tk), :]
        sc = jnp.zeros((tk, tq), jnp.float32)
        for h in range(HI):
            sc = sc + w_t[SMALL_W_OFF + h:SMALL_W_OFF + h + 1, :] * jnp.maximum(
                _dot(ki2, q_heads[h]), 0.0)
        if causal_edge:
            sc = jnp.where(start + k_in_chunk <= q_pos, sc, -jnp.inf)
        key_ref[pl.ds(start, tk), :] = sc
        top = pltpu.bitcast(sc, jnp.int32) & TOP16
        coarse_ref[pl.ds(start, tk), :] = pltpu.bitcast(top, jnp.float32).astype(COARSE_DTYPE)
        return 0

    lax.fori_loop(0, nch - 1, lambda c, _: score_chunk(c, False), 0)
    score_chunk(nch - 1, True)

    def count(pred):
        def body(c, acc):
            start = pl.multiple_of(c * tk, tk)
            ones = jnp.where(pred(key_ref[pl.ds(start, tk), :], start), 1.0, 0.0)
            return acc + jnp.sum(ones.reshape(tk // COUNT_ROWS, COUNT_ROWS, tq), axis=0)
        acc = lax.fori_loop(0, nch, body, jnp.zeros((COUNT_ROWS, tq), jnp.float32))
        return jnp.sum(acc, axis=0, keepdims=True)

    def count_coarse(t):
        one, zero = jnp.ones((), COARSE_DTYPE), jnp.zeros((), COARSE_DTYPE)

        def body(c, acc):
            start = pl.multiple_of(c * tk, tk)
            ones = jnp.where(coarse_ref[pl.ds(start, tk), :] >= t, one, zero)
            parts = [ones[r0:r0 + COUNT_ROWS] for r0 in range(0, tk, COUNT_ROWS)]
            while len(parts) > 1:
                parts = [a + b for a, b in zip(parts[0::2], parts[1::2])]
            return acc + parts[0]
        acc = lax.fori_loop(0, nch, body, jnp.zeros((COUNT_ROWS, tq), COARSE_DTYPE))
        return jnp.sum(acc.astype(jnp.float32), axis=0, keepdims=True)

    def float_of_key(key):
        return pltpu.bitcast(key ^ ((key >> 31) & INT32_MAX), jnp.float32)

    k_sel = jnp.minimum(top_k, q_pos + 1).astype(jnp.float32)
    total = (nch * tk).astype(jnp.float32)

    def coarse_step(_, carry):
        lo16, hi16, n_lo = carry
        mid16 = (lo16 + hi16) >> 1
        key = mid16 << 16
        t = pltpu.bitcast((key ^ ((key >> 31) & INT32_MAX)) & TOP16, jnp.float32)
        n_mid = count_coarse(t.astype(COARSE_DTYPE))
        up = n_mid >= k_sel
        return jnp.where(up, mid16, lo16), jnp.where(up, hi16, mid16), jnp.where(up, n_mid, n_lo)

    lo16, _, n_lo = lax.fori_loop(
        0, 16, coarse_step,
        (jnp.full((1, tq), KEY_NEG_INF >> 16, jnp.int32), jnp.full((1, tq), KEY_POS_INF >> 16, jnp.int32),
         jnp.broadcast_to(total, (1, tq))))
    lo = lo16 << 16
    hi = lo + (1 << 16)

    hi = jnp.where(count(lambda x, start: x > float_of_key(lo)) < k_sel, lo + 1, hi)

    def unsettled(lo, hi, n_lo):
        return jnp.max(jnp.where((n_lo != k_sel) & (hi != lo + 1), 1.0, 0.0)) > 0.0

    def bisect(carry):
        _, lo, hi, n_lo = carry
        for _ in range(BISECT_STEPS_PER_CHECK):
            open_ = (n_lo != k_sel) & (hi != lo + 1)
            mid = (lo & hi) + ((lo ^ hi) >> 1)
            mid_f = float_of_key(mid)
            n_mid = count(lambda x, start: x >= mid_f)
            up = open_ & (n_mid >= k_sel)
            down = open_ & (n_mid < k_sel)
            lo, hi, n_lo = (jnp.where(up, mid, lo), jnp.where(down, mid, hi),
                            jnp.where(up, n_mid, n_lo))
        return unsettled(lo, hi, n_lo), lo, hi, n_lo

    _, thr_key, _, n_ge = lax.while_loop(lambda carry: carry[0], bisect,
                                         (unsettled(lo, hi, n_lo), lo, hi, n_lo))
    thr = float_of_key(thr_key)

    def tie_cutoff():
        need = k_sel - count(lambda x, start: x > thr)

        def step(_, lohi):
            lo, hi = lohi
            mid = (lo + hi) >> 1
            got = count(lambda x, start: (x == thr) & (start + k_in_chunk <= mid))
            ok = got >= need
            return jnp.where(ok, lo, mid), jnp.where(ok, mid, hi)
        steps = max(1, math.ceil(math.log2(seq))) + 1
        _, hi = lax.fori_loop(0, steps, step, (jnp.full((1, tq), -1, jnp.int32),
                                               jnp.full((1, tq), seq - 1, jnp.int32)))
        return hi

    any_tie = jnp.max(jnp.where(n_ge != k_sel, 1.0, 0.0)) > 0.0
    cutoff = lax.cond(any_tie, tie_cutoff, lambda: jnp.full((1, tq), seq - 1, jnp.int32))

    def bias_chunk(c, _):
        start = pl.multiple_of(c * tk, tk)
        x = key_ref[pl.ds(start, tk), :]
        sel = (x > thr) | ((x == thr) & (start + k_in_chunk <= cutoff))
        bias_ref[pl.ds(start, tk), :] = jnp.where(sel, 0.0, MASKED)
        return 0

    lax.fori_loop(0, nch, bias_chunk, 0)

    _chain_reset(scratch)
    for h in range(HB):
        qt_ref[h] = _transposed(qb_ref[0, :, h * DB:(h + 1) * DB])

    def issue(c, slot):
        start = pl.multiple_of(c * tk, tk)
        k = kb_ref[0, pl.ds(start, tk), :]
        bias = bias_ref[pl.ds(start, tk), :]
        for h in range(HB):
            _store_logits(slot, h, _dot(k, qt_ref[h]) + bias, scratch)

    def absorb(c, slot, _):
        start = pl.multiple_of(c * tk, tk)
        _absorb(slot, lambda h: vbt_ref[:, pl.ds(start, tk)], scratch)

    _pipelined_chunks(nch - 1, issue, absorb)
    for h in range(HB):
        o_ref[0, :, h * DB:(h + 1) * DB] = _chain_output(h, scratch).astype(o_ref.dtype)


def _dsa_attention(qi, sm, ki2, qb, kb, vb_t, top_k, tq, tk):
    b, s, _ = qb.shape
    qrow = lambda n: pl.BlockSpec((1, tq, n), lambda bi, i: (bi, i, 0))
    full = lambda n: pl.BlockSpec((1, s, n), lambda bi, i: (bi, 0, 0), pipeline_mode=pl.Buffered(1))
    vspec = pl.BlockSpec((DB, s), lambda bi, i: (0, bi), pipeline_mode=pl.Buffered(1))
    return pl.pallas_call(
        functools.partial(_dsa_kernel, top_k=top_k, tk=tk),
        grid=(b, s // tq),
        in_specs=[qrow(HI * DI), qrow(LANES), full(LANES), qrow(HB * DB), full(DB), vspec],
        out_specs=qrow(HB * DB), out_shape=jax.ShapeDtypeStruct((b, s, HB * DB), MXU_DTYPE),
        scratch_shapes=[pltpu.VMEM((s, tq), jnp.float32), pltpu.VMEM((s, tq), COARSE_DTYPE),
                        pltpu.VMEM((s, tq), jnp.float32), pltpu.VMEM((HB, DB, tq), MXU_DTYPE)]
        + _flash_scratch(HB, tk, DB, tq),
        compiler_params=_params(("parallel", "arbitrary")), name="dsa_attention",
    )(qi, sm, ki2, qb, kb, vb_t)


def _merge_kernel(h_ref, oa_ref, ob_ref, oc_ref, wg_ref, wa_ref, wb_ref, wc_ref, wo_ref, g_ref, b_ref,
                  out_ref):
    h = h_ref[...]
    gates = 1.0 / (1.0 + jnp.exp(-_dot(h.astype(MXU_DTYPE), wg_ref[...])))
    merged = (gates[:, :D_MODEL] * _dot(oa_ref[...], wa_ref[...])
              + gates[:, D_MODEL:2 * D_MODEL] * _dot(ob_ref[...], wb_ref[...])
              + gates[:, 2 * D_MODEL:] * _dot(oc_ref[...], wc_ref[...]))
    mix = _dot(merged.astype(MXU_DTYPE), wo_ref[...])
    out_ref[...] = _layer_norm_rows(ALPHA * h + mix, g_ref[...], b_ref[...])


def _merge(h, oa, ob, oc, lw, tm):
    t = h.shape[0]
    row = lambda n: pl.BlockSpec((tm, n), lambda i: (i, 0))
    weights = [lw["wg"], lw["wbr_a"], lw["wbr_b"], lw["wbr_c"], lw["wo"], lw["ln1_g"], lw["ln1_b"]]
    return pl.pallas_call(
        _merge_kernel, grid=(t // tm,),
        in_specs=[row(D_MODEL), row(HA * DA), row(HB * DB), row(HC * DCV)]
        + [_const_spec(w.shape) for w in weights],
        out_specs=row(D_MODEL), out_shape=jax.ShapeDtypeStruct((t, D_MODEL), jnp.float32),
        compiler_params=_params(("parallel",)), name="merge",
    )(h, oa, ob, oc, *weights)


FF_CHUNK = D_FF // 2
CONV_HALO = SUBLANES


def _ffn_kernel(h_ref, wup_ref, cw_ref, cb_ref, wdn_ref, g_ref, b_ref, out_ref, gate_ref,
                *, tiles_per_seq):
    i = pl.program_id(0)
    tm = h_ref.shape[0]
    h = h_ref[...]
    hb = h.astype(MXU_DTYPE)

    @pl.when(i % tiles_per_seq == 0)
    def _():
        gate_ref[0:CONV_HALO, :] = jnp.zeros((CONV_HALO, D_FF), jnp.float32)

    ffn = jnp.zeros((tm, D_MODEL), jnp.float32)
    for f0 in range(0, D_FF, FF_CHUNK):
        f1 = f0 + FF_CHUNK
        gate_ref[CONV_HALO:, f0:f1] = _dot(hb, wup_ref[:, f0:f1])
        val = _dot(hb, wup_ref[:, D_FF + f0:D_FF + f1])
        conv = cb_ref[:, f0:f1]
        for j in range(CONV_W):
            off = CONV_HALO - (CONV_W - 1) + j
            conv = conv + cw_ref[j:j + 1, f0:f1] * gate_ref[off:off + tm, f0:f1]
        act = conv / (1.0 + jnp.exp(-conv)) * val
        ffn = ffn + _dot(act.astype(MXU_DTYPE), wdn_ref[f0:f1, :])
    gate_ref[0:CONV_HALO, :] = gate_ref[tm:tm + CONV_HALO, :]
    out_ref[...] = _layer_norm_rows(ALPHA * h + ffn, g_ref[...], b_ref[...])


def _conv_gated_mlp(h, lw, seq, tm):
    t = h.shape[0]
    row = pl.BlockSpec((tm, D_MODEL), lambda i: (i, 0))
    weights = [lw["wup"], lw["conv_w"], lw["conv_b"], lw["wdn"], lw["ln2_g"], lw["ln2_b"]]
    return pl.pallas_call(
        functools.partial(_ffn_kernel, tiles_per_seq=seq // tm), grid=(t // tm,),
        in_specs=[row] + [_const_spec(w.shape) for w in weights],
        out_specs=row, out_shape=jax.ShapeDtypeStruct((t, D_MODEL), jnp.float32),
        scratch_shapes=[pltpu.VMEM((tm + CONV_HALO, D_FF), jnp.float32)],
        compiler_params=_params(("arbitrary",)), name="conv_gated_mlp",
    )(h, *weights)


def _layer_weights(l, w_in, b_f, kv_norm_g, w_ukv, w_gate, w_br_a, w_br_b, w_br_c, w_out, ln1_g, ln1_b,
                   w_up, conv_w, conv_b, w_down, ln2_g, ln2_b):
    widths = (HA * DA, HA * DA, HA * DA, HA, HB * DB, KV_LATENT, HI * DI, DI, HI,
              HC * 2 * DC, HC * 2 * DC, HC * DCV)
    offs = [0]
    for n in widths:
        offs.append(offs[-1] + n)
    col = lambda k: w_in[l][:, offs[k]:offs[k + 1]]
    aq, ak, av, af, bq, bc, biq, bik, biw, cq, ck, cv = (col(k) for k in range(len(widths)))
    pad = jnp.zeros((D_MODEL, LANES - HI - HA - DI), w_in.dtype)
    mx = lambda a: a.astype(MXU_DTYPE)
    vec = lambda a: a.reshape(1, -1).astype(jnp.float32)
    bf = jnp.zeros((1, LANES), jnp.float32).at[0, SMALL_F_OFF:SMALL_F_OFF + HA].set(b_f[l])
    return dict(
        wa=mx(jnp.concatenate([aq, ak, av], axis=1)), wbq=mx(bq), wbc=mx(bc), wiq=mx(biq),
        wc=mx(jnp.concatenate([cq, ck, cv], axis=1)),
        wsm=mx(jnp.concatenate([biw, af, pad, bik], axis=1)), bf=bf, kvg=vec(kv_norm_g[l]),
        wukv=mx(w_ukv[l]), wg=mx(w_gate[l]), wbr_a=mx(w_br_a[l]), wbr_b=mx(w_br_b[l]),
        wbr_c=mx(w_br_c[l]), wo=mx(w_out[l]), ln1_g=vec(ln1_g[l]), ln1_b=vec(ln1_b[l]),
        wup=mx(w_up[l]), conv_w=conv_w[l].astype(jnp.float32), conv_b=vec(conv_b[l]),
        wdn=mx(w_down[l]), ln2_g=vec(ln2_g[l]), ln2_b=vec(ln2_b[l]))


def _tile(n, want):
    t = min(n, want)
    assert n % t == 0, (n, t)
    return t


def kernel(x, positions, ln_in_g, ln_in_b, w_in, b_f, kv_norm_g, w_ukv, lam_qk, diff_norm_g, w_gate,
           w_br_a, w_br_b, w_br_c, w_out, ln1_g, ln1_b, w_up, conv_w, conv_b, w_down, ln2_g, ln2_b):
    b, s, d = x.shape
    assert d == D_MODEL and s % LANES == 0
    t = b * s
    depth = w_in.shape[0]
    top_k = min(TOPK_MAX, s // 4)
    tm = _tile(s, 512)
    tq = _tile(s, 256)
    tk_dsa = _tile(s, 512)

    h = _input_layer_norm(x.reshape(t, d), ln_in_g, ln_in_b, tm)
    pos_lanes = jnp.broadcast_to(positions.reshape(t, 1).astype(jnp.float32), (t, LANES))
    tables = _rope_tables(pos_lanes, tm)

    for l in range(depth):
        lam_init = 0.8 - 0.6 * math.exp(-0.3 * l)
        lw = _layer_weights(l, w_in, b_f, kv_norm_g, w_ukv, w_gate, w_br_a, w_br_b, w_br_c, w_out,
                            ln1_g, ln1_b, w_up, conv_w, conv_b, w_down, ln2_g, ln2_b)
        (qa, ka, va_t, qb, kb, vb_t, qi, ki2, sm, qc, kc, vc_t) = _projections(h, tables, lw, s, tm)
        seq3 = lambda a: a.reshape(b, s, a.shape[-1])
        o_a = _fox_attention(seq3(qa), seq3(ka), va_t, tq, FOX_HEADS)
        o_b = _dsa_attention(seq3(qi), seq3(sm), seq3(ki2), seq3(qb), seq3(kb), vb_t,
                             top_k, tq, tk_dsa)
        o_c = _diff_attention(seq3(qc), seq3(kc), vc_t, lam_qk[l].astype(jnp.float32),
                              diff_norm_g[l].astype(jnp.float32), lam_init, tq, DIFF_HEADS)
        h = _merge(h, o_a.reshape(t, -1), o_b.reshape(t, -1), o_c.reshape(t, -1), lw, tm)
        h = _conv_gated_mlp(h, lw, s, tm)
    return h.reshape(b, s, d)
```

```python
import functools
import math

import jax
import jax.numpy as jnp
from jax import lax
from jax.experimental import pallas as pl
from jax.experimental.pallas import tpu as pltpu

D_MODEL = 1024
ROPE_THETA = 500000.0
HA, DA = 4, 128
HB, DB = 4, 128
KV_LATENT = 256
HI, DI = 8, 64
TOPK_MAX = 256
HC, DC, DCV = 4, 64, 128
D_FF = 2816
CONV_W = 3
MODEL_DEPTH = 4
ALPHA = (2 * MODEL_DEPTH) ** 0.25
LN_EPS = 1e-5
RMS_EPS = 1e-6

LANES = 128
SUBLANES = 8
VMEM_LIMIT_BYTES = 56 * 1024 * 1024
MXU_DTYPE = jnp.bfloat16
MASKED = -1e30
LOG2E = math.log2(math.e)
INT32_MIN = -2 ** 31
INT32_MAX = 2 ** 31 - 1


def _dot(a, b):
    return jnp.dot(a, b, preferred_element_type=jnp.float32)


def _layer_norm_rows(x, g, b):
    mu = jnp.mean(x, axis=-1, keepdims=True)
    xc = x - mu
    var = jnp.mean(xc * xc, axis=-1, keepdims=True)
    return xc * lax.rsqrt(var + LN_EPS) * g + b


def _params(semantics):
    return pltpu.CompilerParams(dimension_semantics=semantics, vmem_limit_bytes=VMEM_LIMIT_BYTES)


def _const_spec(shape):
    nd = len(shape)
    return pl.BlockSpec(shape, lambda *_: (0,) * nd, pipeline_mode=pl.Buffered(1))


def _ln_kernel(x_ref, g_ref, b_ref, o_ref):
    o_ref[...] = _layer_norm_rows(x_ref[...], g_ref[...], b_ref[...])


def _input_layer_norm(x2d, g, b, tm):
    t, d = x2d.shape
    row = pl.BlockSpec((tm, d), lambda i: (i, 0))
    return pl.pallas_call(
        _ln_kernel, grid=(t // tm,), in_specs=[row, _const_spec((1, d)), _const_spec((1, d))],
        out_specs=row, out_shape=jax.ShapeDtypeStruct((t, d), jnp.float32),
        compiler_params=_params(("parallel",)), name="input_ln",
    )(x2d, g.reshape(1, d), b.reshape(1, d))


def _rope_table_kernel(pos_ref, f16_ref, g16_ref, f8_ref, g8_ref, c16_ref, s16_ref, c8_ref, s8_ref):
    p = pos_ref[...]
    a16 = p * f16_ref[...]
    c16_ref[...] = jnp.cos(a16)
    s16_ref[...] = jnp.sin(a16) * g16_ref[...]
    a8 = p * f8_ref[...]
    c8_ref[...] = jnp.cos(a8)
    s8_ref[...] = jnp.sin(a8) * g8_ref[...]


def _lane_freq_sign(head_dim):
    rot = head_dim // 4
    half = rot // 2
    inv_freq = ROPE_THETA ** (-jnp.arange(half, dtype=jnp.float32) / half)
    d = jnp.arange(LANES) % head_dim
    freq = jnp.where(d < rot, inv_freq[d % half], 0.0).astype(jnp.float32)
    sign = jnp.where(d < half, -1.0, jnp.where(d < rot, 1.0, 0.0)).astype(jnp.float32)
    return freq.reshape(1, LANES), sign.reshape(1, LANES)


def _rope_tables(pos_lanes, tm):
    t = pos_lanes.shape[0]
    f16, g16 = _lane_freq_sign(DB)
    f8, g8 = _lane_freq_sign(DI)
    row = pl.BlockSpec((tm, LANES), lambda i: (i, 0))
    vec = _const_spec((1, LANES))
    tab = jax.ShapeDtypeStruct((t, LANES), jnp.float32)
    return pl.pallas_call(
        _rope_table_kernel, grid=(t // tm,), in_specs=[row, vec, vec, vec, vec],
        out_specs=[row] * 4, out_shape=[tab] * 4,
        compiler_params=_params(("parallel",)), name="rope_tables",
    )(pos_lanes, f16, g16, f8, g8)


def _rope_chunk(x, cos, sin_signed, first_half, half):
    partner = jnp.where(first_half, pltpu.roll(x, LANES - half, axis=1), pltpu.roll(x, half, axis=1))
    return x * cos + partner * sin_signed


SMALL_W_OFF, SMALL_F_OFF, SMALL_KI_OFF = 0, HI, LANES - DI


def _proj_kernel(h_ref, c16_ref, s16_ref, c8_ref, s8_ref, wa_ref, wbq_ref, wbc_ref, wiq_ref, wc_ref,
                 wsm_ref, bf_ref, kvg_ref, wukv_ref,
                 qa_ref, ka_ref, vat_ref, qb_ref, kb_ref, vbt_ref, qi_ref, ki2_ref, sm_ref,
                 qc_ref, kc_ref, vct_ref, carry_ref, *, tiles_per_seq):
    i = pl.program_id(0)
    tm = h_ref.shape[0]
    hb = h_ref[...].astype(MXU_DTYPE)
    lane = lax.broadcasted_iota(jnp.int32, (1, LANES), 1)
    first16 = lane < (DB // 8)
    first8 = (lane % DI) < (DI // 8)
    c16, s16, c8, s8 = c16_ref[...], s16_ref[...], c8_ref[...], s8_ref[...]
    rope16 = lambda v: _rope_chunk(v, c16, s16, first16, DB // 8)
    rope8 = lambda v: _rope_chunk(v, c8, s8, first8, DI // 8)
    odt = qa_ref.dtype

    ya = _dot(hb, wa_ref[...])
    qa_ref[...] = (ya[:, :HA * DA] * (DA ** -0.5 * LOG2E)).astype(odt)
    for c in range(HA):
        ka_ref[:, 2 * c * DA:(2 * c + 1) * DA] = ya[:, (HA + c) * DA:(HA + c + 1) * DA].astype(odt)
    vat_ref[...] = ya[:, 2 * HA * DA:].T.astype(odt)

    yq = _dot(hb, wbq_ref[...])
    for c in range(HB):
        qb_ref[:, c * LANES:(c + 1) * LANES] = (
            rope16(yq[:, c * LANES:(c + 1) * LANES]) * (DB ** -0.5 * LOG2E)).astype(odt)
    lat = _dot(hb, wbc_ref[...])
    lat = lat * lax.rsqrt(jnp.mean(lat * lat, axis=-1, keepdims=True) + RMS_EPS) * kvg_ref[...]
    kv = _dot(lat.astype(MXU_DTYPE), wukv_ref[...])
    kb_ref[...] = rope16(kv[:, :DB]).astype(odt)
    vbt_ref[...] = kv[:, DB:].T.astype(odt)

    yi = _dot(hb, wiq_ref[...])
    for c in range(HI * DI // LANES):
        qi_ref[:, c * LANES:(c + 1) * LANES] = (
            rope8(yi[:, c * LANES:(c + 1) * LANES]) * (DI ** -0.5)).astype(odt)

    ysm = _dot(hb, wsm_ref[...])
    sm_ref[...] = ysm * (HI ** -0.5)
    ki = rope8(ysm)
    ki2_ref[...] = jnp.where(lane < SMALL_KI_OFF, pltpu.roll(ki, DI, axis=1), ki).astype(odt)

    z = ysm + bf_ref[...]
    logf = jnp.minimum(z, 0.0) - jnp.log1p(jnp.exp(-jnp.abs(z)))
    r = lax.broadcasted_iota(jnp.int32, (tm, tm), 0)
    cidx = lax.broadcasted_iota(jnp.int32, (tm, tm), 1)
    tri = jnp.where(r >= cidx, 1.0, 0.0).astype(jnp.float32)
    local = jnp.dot(tri, logf, preferred_element_type=jnp.float32, precision=lax.Precision.HIGHEST)

    @pl.when(i % tiles_per_seq == 0)
    def _():
        carry_ref[...] = jnp.zeros_like(carry_ref)

    cum = local + carry_ref[...]
    carry_ref[...] = cum[tm - 1:tm, :]
    for c in range(HA):
        bias = cum[:, SMALL_F_OFF + c:SMALL_F_OFF + c + 1] * (-LOG2E)
        hi = bias.astype(odt).astype(jnp.float32)
        mid = (bias - hi).astype(odt).astype(jnp.float32)
        lo = bias - hi - mid
        pieces = jnp.where(lane == 0, hi, jnp.where(lane == 1, mid, jnp.where(lane == 2, lo, 0.0)))
        ka_ref[:, (2 * c + 1) * DA:(2 * c + 2) * DA] = pieces.astype(odt)

    yc = _dot(hb, wc_ref[...])
    nqc = HC * 2 * DC
    for c in range(nqc // LANES):
        qc_ref[:, c * LANES:(c + 1) * LANES] = (
            rope8(yc[:, c * LANES:(c + 1) * LANES]) * (DC ** -0.5 * LOG2E)).astype(odt)
        kc_ref[:, c * LANES:(c + 1) * LANES] = rope8(
            yc[:, nqc + c * LANES:nqc + (c + 1) * LANES]).astype(odt)
    vct_ref[...] = yc[:, 2 * nqc:].T.astype(odt)


def _projections(h, tables, lw, seq, tm):
    t = h.shape[0]
    row = lambda n: pl.BlockSpec((tm, n), lambda i: (i, 0))
    col = lambda n: pl.BlockSpec((n, tm), lambda i: (0, i))
    outs = [(HA * DA, MXU_DTYPE, 0), (HA * 2 * DA, MXU_DTYPE, 0), (HA * DA, MXU_DTYPE, 1),
            (HB * DB, MXU_DTYPE, 0), (DB, MXU_DTYPE, 0), (DB, MXU_DTYPE, 1),
            (HI * DI, MXU_DTYPE, 0), (LANES, MXU_DTYPE, 0), (LANES, jnp.float32, 0),
            (HC * 2 * DC, MXU_DTYPE, 0), (HC * 2 * DC, MXU_DTYPE, 0), (HC * DCV, MXU_DTYPE, 1)]
    weights = [lw["wa"], lw["wbq"], lw["wbc"], lw["wiq"], lw["wc"], lw["wsm"], lw["bf"], lw["kvg"],
               lw["wukv"]]
    return pl.pallas_call(
        functools.partial(_proj_kernel, tiles_per_seq=seq // tm),
        grid=(t // tm,),
        in_specs=[row(D_MODEL)] + [row(LANES)] * 4 + [_const_spec(w.shape) for w in weights],
        out_specs=[col(n) if ax else row(n) for n, _, ax in outs],
        out_shape=[jax.ShapeDtypeStruct((n, t) if ax else (t, n), dt) for n, dt, ax in outs],
        scratch_shapes=[pltpu.VMEM((1, LANES), jnp.float32)],
        compiler_params=_params(("arbitrary",)), name="projections",
    )(h, *tables, *weights)


STRIP = 32
SUM_ROWS = 16


def _flash_scratch(chains, keys, width, tq):
    return [pltpu.VMEM((2, chains, keys, tq), jnp.float32),
            pltpu.VMEM((2, chains, SUBLANES, tq), jnp.float32),
            pltpu.VMEM((chains, keys, tq), MXU_DTYPE),
            pltpu.VMEM((chains, 1, tq), jnp.float32),
            pltpu.VMEM((chains, width + SUM_ROWS, tq), jnp.float32)]


def _chain_reset(scratch):
    m_ref, acc_ref = scratch[3], scratch[4]
    m_ref[...] = jnp.full(m_ref.shape, -jnp.inf, jnp.float32)
    acc_ref[...] = jnp.zeros(acc_ref.shape, jnp.float32)


def _fold(x, op):
    rows, tq = x.shape
    return op(x.reshape(rows // SUBLANES, SUBLANES, tq), axis=0)


def _store_logits(slot, c, s, scratch):
    s_ref, mp_ref = scratch[0], scratch[1]
    s_ref[slot, c] = s
    part = _fold(s[0:STRIP], jnp.max)
    for r0 in range(STRIP, s.shape[0], STRIP):
        part = jnp.maximum(part, _fold(s[r0:r0 + STRIP], jnp.max))
    mp_ref[slot, c] = part


def _absorb(slot, vt_of, scratch, diagonal=False):
    s_ref, mp_ref, p_ref, m_ref, acc_ref = scratch
    _, chains, keys, tq = s_ref.shape
    alphas = []
    for c in range(chains):
        def strip(r0):
            x = s_ref[slot, c, r0:r0 + STRIP, :]
            if diagonal:
                kk = r0 + lax.broadcasted_iota(jnp.int32, (STRIP, tq), 0)
                x = jnp.where(kk <= lax.broadcasted_iota(jnp.int32, (STRIP, tq), 1), x, -jnp.inf)
            return x

        if diagonal:
            m_part = _fold(strip(0), jnp.max)
            for r0 in range(STRIP, keys, STRIP):
                m_part = jnp.maximum(m_part, _fold(strip(r0), jnp.max))
        else:
            m_part = mp_ref[slot, c]
        m_old = m_ref[c]
        m_new = jnp.maximum(m_old, jnp.max(m_part, axis=0, keepdims=True))
        alphas.append(jnp.exp2(m_old - m_new))
        m_ref[c] = m_new
        for r0 in range(0, keys, STRIP):
            p_ref[c, r0:r0 + STRIP, :] = jnp.exp2(strip(r0) - m_new).astype(p_ref.dtype)
    ones = jnp.ones((SUM_ROWS, keys), p_ref.dtype)
    for c in range(chains):
        acc_ref[c] = alphas[c] * acc_ref[c] + _dot(jnp.concatenate([vt_of(c), ones], axis=0), p_ref[c])


def _chain_output(c, scratch):
    acc_ref = scratch[4]
    width = acc_ref.shape[1] - SUM_ROWS
    return (acc_ref[c, :width, :] / acc_ref[c, width:width + 1, :]).T


def _pipelined_chunks(last, issue, absorb):
    issue(last, 0)
    issue(0, 1)
    absorb(last, 0, True)

    def pair(jj, _):
        j = 2 * jj
        issue(j + 1, 0)
        absorb(j, 1, False)
        issue(jnp.minimum(j + 2, last), 1)
        absorb(j + 1, 0, False)
        return 0

    lax.fori_loop(0, last // 2, pair, 0)

    @pl.when(last % 2 == 1)
    def _():
        absorb(last - 1, 1, False)


def _transposed(x):
    return x.astype(jnp.float32).T.astype(MXU_DTYPE)


FOX_HEADS = 4
DIFF_HEADS = 4


def _fox_kernel(q_ref, k_ref, vt_ref, o_ref, qt_ref, *scratch, heads):
    qi = pl.program_id(2)
    tq = q_ref.shape[1]
    _chain_reset(scratch)
    for g in range(heads):
        qt_ref[g, :DA, :] = _transposed(q_ref[0, :, g * DA:(g + 1) * DA])
        qt_ref[g, DA:, :] = jnp.ones((DA, tq), MXU_DTYPE)

    def issue(j, slot):
        start = pl.multiple_of(j * tq, tq)
        for g in range(heads):
            k = k_ref[0, pl.ds(start, tq), g * 2 * DA:(g + 1) * 2 * DA]
            _store_logits(slot, g, _dot(k, qt_ref[g]), scratch)

    def absorb(j, slot, diagonal):
        start = pl.multiple_of(j * tq, tq)
        _absorb(slot, lambda g: vt_ref[g * DA:(g + 1) * DA, pl.ds(start, tq)], scratch,
                diagonal=diagonal)

    _pipelined_chunks(qi, issue, absorb)
    for g in range(heads):
        o_ref[0, :, g * DA:(g + 1) * DA] = _chain_output(g, scratch).astype(o_ref.dtype)


def _fox_attention(qa, kaug, va_t, tq, heads):
    b, s, _ = qa.shape
    qspec = pl.BlockSpec((1, tq, heads * DA), lambda bi, h, i: (bi, i, h))
    kspec = pl.BlockSpec((1, s, heads * 2 * DA), lambda bi, h, i: (bi, 0, h),
                         pipeline_mode=pl.Buffered(1))
    vspec = pl.BlockSpec((heads * DA, s), lambda bi, h, i: (h, bi), pipeline_mode=pl.Buffered(1))
    return pl.pallas_call(
        functools.partial(_fox_kernel, heads=heads),
        grid=(b, HA // heads, s // tq), in_specs=[qspec, kspec, vspec], out_specs=qspec,
        out_shape=jax.ShapeDtypeStruct((b, s, HA * DA), MXU_DTYPE),
        scratch_shapes=[pltpu.VMEM((heads, 2 * DA, tq), MXU_DTYPE)] + _flash_scratch(heads, tq, DA, tq),
        compiler_params=_params(("parallel", "parallel", "arbitrary")), name="fox_attention",
    )(qa, kaug, va_t)


def _diff_kernel(q_ref, k_ref, vt_ref, lq_ref, g_ref, o_ref, qt_ref, *scratch, heads, lam_init):
    qi = pl.program_id(2)
    tq = q_ref.shape[1]
    _chain_reset(scratch)
    row = lax.broadcasted_iota(jnp.int32, (2 * DC, 1), 0)
    for g in range(heads):
        both = _transposed(q_ref[0, :, g * 2 * DC:(g + 1) * 2 * DC])
        zero = jnp.zeros_like(both)
        qt_ref[2 * g] = jnp.where(row < DC, both, zero)
        qt_ref[2 * g + 1] = jnp.where(row >= DC, both, zero)

    def issue(j, slot):
        start = pl.multiple_of(j * tq, tq)
        for c in range(2 * heads):
            g = c // 2
            k = k_ref[0, pl.ds(start, tq), g * 2 * DC:(g + 1) * 2 * DC]
            _store_logits(slot, c, _dot(k, qt_ref[c]), scratch)

    def absorb(j, slot, diagonal):
        start = pl.multiple_of(j * tq, tq)
        _absorb(slot, lambda c: vt_ref[(c // 2) * DCV:(c // 2 + 1) * DCV, pl.ds(start, tq)], scratch,
                diagonal=diagonal)

    _pipelined_chunks(qi, issue, absorb)
    lq = lq_ref[...]
    lam = (jnp.exp(jnp.sum(lq[0:1] * lq[1:2], axis=-1, keepdims=True))
           - jnp.exp(jnp.sum(lq[2:3] * lq[3:4], axis=-1, keepdims=True)) + lam_init)
    for g in range(heads):
        o = _chain_output(2 * g, scratch) - lam * _chain_output(2 * g + 1, scratch)
        o = o * lax.rsqrt(jnp.mean(o * o, axis=-1, keepdims=True) + RMS_EPS) * g_ref[...]
        o_ref[0, :, g * DCV:(g + 1) * DCV] = (o * (1.0 - lam_init)).astype(o_ref.dtype)


def _diff_attention(qc, kc, vc_t, lam_qk, norm_g, lam_init, tq, heads):
    b, s, _ = qc.shape
    qspec = pl.BlockSpec((1, tq, heads * DCV), lambda bi, h, i: (bi, i, h))
    kspec = pl.BlockSpec((1, s, heads * 2 * DC), lambda bi, h, i: (bi, 0, h),
                         pipeline_mode=pl.Buffered(1))
    vspec = pl.BlockSpec((heads * DCV, s), lambda bi, h, i: (h, bi), pipeline_mode=pl.Buffered(1))
    return pl.pallas_call(
        functools.partial(_diff_kernel, heads=heads, lam_init=lam_init),
        grid=(b, HC // heads, s // tq),
        in_specs=[qspec, kspec, vspec, _const_spec((4, DC)), _const_spec((1, DCV))],
        out_specs=qspec, out_shape=jax.ShapeDtypeStruct((b, s, HC * DCV), MXU_DTYPE),
        scratch_shapes=[pltpu.VMEM((2 * heads, 2 * DC, tq), MXU_DTYPE)]
        + _flash_scratch(2 * heads, tq, DCV, tq),
        compiler_params=_params(("parallel", "parallel", "arbitrary")), name="diff_attention",
    )(qc, kc, vc_t, lam_qk, norm_g.reshape(1, DCV))


COUNT_ROWS = 4 * SUBLANES
BISECT_STEPS_PER_CHECK = 4
COARSE_DTYPE = jnp.bfloat16
TOP16 = -(1 << 16)
KEY_NEG_INF = (-(1 << 23)) ^ INT32_MAX
KEY_POS_INF = 0x7F800000


def _dsa_kernel(qi_ref, w_ref, ki2_ref, qb_ref, kb_ref, vbt_ref, o_ref, key_ref, coarse_ref,
                qt_ref, *scratch, top_k, tk):
    qt = pl.program_id(1)
    tq = qi_ref.shape[1]
    seq = key_ref.shape[0]
    nch = (qt * tq) // tk + 1
    q_pos = qt * tq + lax.broadcasted_iota(jnp.int32, (1, tq), 1)
    k_in_chunk = lax.broadcasted_iota(jnp.int32, (tk, 1), 0)
    row = lax.broadcasted_iota(jnp.int32, (2 * DI, 1), 0)

    w_t = w_ref[0].T
    q_heads = []
    for c in range(HI * DI // LANES):
        pair = _transposed(qi_ref[0, :, c * LANES:(c + 1) * LANES])
        zero = jnp.zeros_like(pair)
        q_heads += [jnp.where(row < DI, pair, zero), jnp.where(row >= DI, pair, zero)]

    def score_chunk(c, causal_edge):
        start = pl.multiple_of(c * tk, tk)
        ki2 = ki2_ref[0, pl.ds(start, tk), :]
        sc = jnp.zeros((tk, tq), jnp.float32)
        for h in range(HI):
            sc = sc + w_t[SMALL_W_OFF + h:SMALL_W_OFF + h + 1, :] * jnp.maximum(
                _dot(ki2, q_heads[h]), 0.0)
        if causal_edge:
            sc = jnp.where(start + k_in_chunk <= q_pos, sc, -jnp.inf)
        key_ref[pl.ds(start, tk), :] = sc
        top = pltpu.bitcast(sc, jnp.int32) & TOP16
        coarse_ref[pl.ds(start, tk), :] = pltpu.bitcast(top, jnp.float32).astype(COARSE_DTYPE)
        return 0

    lax.fori_loop(0, nch - 1, lambda c, _: score_chunk(c, False), 0)
    score_chunk(nch - 1, True)

    def count(pred):
        def body(c, acc):
            start = pl.multiple_of(c * tk, tk)
            ones = jnp.where(pred(key_ref[pl.ds(start, tk), :], start), 1.0, 0.0)
            return acc + jnp.sum(ones.reshape(tk // COUNT_ROWS, COUNT_ROWS, tq), axis=0)
        acc = lax.fori_loop(0, nch, body, jnp.zeros((COUNT_ROWS, tq), jnp.float32))
        return jnp.sum(acc, axis=0, keepdims=True)

    def count_coarse(t):
        one, zero = jnp.ones((), COARSE_DTYPE), jnp.zeros((), COARSE_DTYPE)

        def body(c, acc):
            start = pl.multiple_of(c * tk, tk)
            ones = jnp.where(coarse_ref[pl.ds(start, tk), :] >= t, one, zero)
            parts = [ones[r0:r0 + COUNT_ROWS] for r0 in range(0, tk, COUNT_ROWS)]
            while len(parts) > 1:
                parts = [a + b for a, b in zip(parts[0::2], parts[1::2])]
            return acc + parts[0]
        acc = lax.fori_loop(0, nch, body, jnp.zeros((COUNT_ROWS, tq), COARSE_DTYPE))
        return jnp.sum(acc.astype(jnp.float32), axis=0, keepdims=True)

    def float_of_key(key):
        return pltpu.bitcast(key ^ ((key >> 31) & INT32_MAX), jnp.float32)

    k_sel = jnp.minimum(top_k, q_pos + 1).astype(jnp.float32)
    total = (nch * tk).astype(jnp.float32)

    def coarse_step(_, carry):
        lo16, hi16, n_lo = carry
        mid16 = (lo16 + hi16) >> 1
        key = mid16 << 16
        t = pltpu.bitcast((key ^ ((key >> 31) & INT32_MAX)) & TOP16, jnp.float32)
        n_mid = count_coarse(t.astype(COARSE_DTYPE))
        up = n_mid >= k_sel
        return jnp.where(up, mid16, lo16), jnp.where(up, hi16, mid16), jnp.where(up, n_mid, n_lo)

    lo16, _, n_lo = lax.fori_loop(
        0, 16, coarse_step,
        (jnp.full((1, tq), KEY_NEG_INF >> 16, jnp.int32), jnp.full((1, tq), KEY_POS_INF >> 16, jnp.int32),
         jnp.broadcast_to(total, (1, tq))))
    lo = lo16 << 16
    hi = lo + (1 << 16)

    hi = jnp.where(count(lambda x, start: x > float_of_key(lo)) < k_sel, lo + 1, hi)

    def unsettled(lo, hi, n_lo):
        return jnp.max(jnp.where((n_lo != k_sel) & (hi != lo + 1), 1.0, 0.0)) > 0.0

    def bisect(carry):
        _, lo, hi, n_lo = carry
        for _ in range(BISECT_STEPS_PER_CHECK):
            open_ = (n_lo != k_sel) & (hi != lo + 1)
            mid = (lo & hi) + ((lo ^ hi) >> 1)
            mid_f = float_of_key(mid)
            n_mid = count(lambda x, start: x >= mid_f)
            up = open_ & (n_mid >= k_sel)
            down = open_ & (n_mid < k_sel)
            lo, hi, n_lo = (jnp.where(up, mid, lo), jnp.where(down, mid, hi),
                            jnp.where(up, n_mid, n_lo))
        return unsettled(lo, hi, n_lo), lo, hi, n_lo

    _, thr_key, _, n_ge = lax.while_loop(lambda carry: carry[0], bisect,
                                         (unsettled(lo, hi, n_lo), lo, hi, n_lo))
    thr = float_of_key(thr_key)

    def tie_cutoff():
        need = k_sel - count(lambda x, start: x > thr)

        def step(_, lohi):
            lo, hi = lohi
            mid = (lo + hi) >> 1
            got = count(lambda x, start: (x == thr) & (start + k_in_chunk <= mid))
            ok = got >= need
            return jnp.where(ok, lo, mid), jnp.where(ok, mid, hi)
        steps = max(1, math.ceil(math.log2(seq))) + 1
        _, hi = lax.fori_loop(0, steps, step, (jnp.full((1, tq), -1, jnp.int32),
                                               jnp.full((1, tq), seq - 1, jnp.int32)))
        return hi

    any_tie = jnp.max(jnp.where(n_ge != k_sel, 1.0, 0.0)) > 0.0
    cutoff = lax.cond(any_tie, tie_cutoff, lambda: jnp.full((1, tq), seq - 1, jnp.int32))

    _chain_reset(scratch)
    for h in range(HB):
        qt_ref[h] = _transposed(qb_ref[0, :, h * DB:(h + 1) * DB])

    def issue(c, slot):
        start = pl.multiple_of(c * tk, tk)
        k = kb_ref[0, pl.ds(start, tk), :]
        x = key_ref[pl.ds(start, tk), :]
        sel = (x > thr) | ((x == thr) & (start + k_in_chunk <= cutoff))
        bias = jnp.where(sel, 0.0, MASKED)
        for h in range(HB):
            _store_logits(slot, h, _dot(k, qt_ref[h]) + bias, scratch)

    def absorb(c, slot, _):
        start = pl.multiple_of(c * tk, tk)
        _absorb(slot, lambda h: vbt_ref[:, pl.ds(start, tk)], scratch)

    _pipelined_chunks(nch - 1, issue, absorb)
    for h in range(HB):
        o_ref[0, :, h * DB:(h + 1) * DB] = _chain_output(h, scratch).astype(o_ref.dtype)


def _dsa_attention(qi, sm, ki2, qb, kb, vb_t, top_k, tq, tk):
    b, s, _ = qb.shape
    qrow = lambda n: pl.BlockSpec((1, tq, n), lambda bi, i: (bi, i, 0))
    full = lambda n: pl.BlockSpec((1, s, n), lambda bi, i: (bi, 0, 0), pipeline_mode=pl.Buffered(1))
    vspec = pl.BlockSpec((DB, s), lambda bi, i: (0, bi), pipeline_mode=pl.Buffered(1))
    return pl.pallas_call(
        functools.partial(_dsa_kernel, top_k=top_k, tk=tk),
        grid=(b, s // tq),
        in_specs=[qrow(HI * DI), qrow(LANES), full(LANES), qrow(HB * DB), full(DB), vspec],
        out_specs=qrow(HB * DB), out_shape=jax.ShapeDtypeStruct((b, s, HB * DB), MXU_DTYPE),
        scratch_shapes=[pltpu.VMEM((s, tq), jnp.float32), pltpu.VMEM((s, tq), COARSE_DTYPE),
                        pltpu.VMEM((HB, DB, tq), MXU_DTYPE)]
        + _flash_scratch(HB, tk, DB, tq),
        compiler_params=_params(("parallel", "arbitrary")), name="dsa_attention",
    )(qi, sm, ki2, qb, kb, vb_t)


def _merge_kernel(h_ref, oa_ref, ob_ref, oc_ref, wg_ref, wa_ref, wb_ref, wc_ref, wo_ref, g_ref, b_ref,
                  out_ref):
    h = h_ref[...]
    gates = 1.0 / (1.0 + jnp.exp(-_dot(h.astype(MXU_DTYPE), wg_ref[...])))
    merged = (gates[:, :D_MODEL] * _dot(oa_ref[...], wa_ref[...])
              + gates[:, D_MODEL:2 * D_MODEL] * _dot(ob_ref[...], wb_ref[...])
              + gates[:, 2 * D_MODEL:] * _dot(oc_ref[...], wc_ref[...]))
    mix = _dot(merged.astype(MXU_DTYPE), wo_ref[...])
    out_ref[...] = _layer_norm_rows(ALPHA * h + mix, g_ref[...], b_ref[...])


def _merge(h, oa, ob, oc, lw, tm):
    t = h.shape[0]
    row = lambda n: pl.BlockSpec((tm, n), lambda i: (i, 0))
    weights = [lw["wg"], lw["wbr_a"], lw["wbr_b"], lw["wbr_c"], lw["wo"], lw["ln1_g"], lw["ln1_b"]]
    return pl.pallas_call(
        _merge_kernel, grid=(t // tm,),
        in_specs=[row(D_MODEL), row(HA * DA), row(HB * DB), row(HC * DCV)]
        + [_const_spec(w.shape) for w in weights],
        out_specs=row(D_MODEL), out_shape=jax.ShapeDtypeStruct((t, D_MODEL), jnp.float32),
        compiler_params=_params(("parallel",)), name="merge",
    )(h, oa, ob, oc, *weights)


FF_CHUNK = D_FF // 2
CONV_HALO = SUBLANES


def _ffn_kernel(h_ref, wup_ref, cw_ref, cb_ref, wdn_ref, g_ref, b_ref, out_ref, gate_ref,
                *, tiles_per_seq):
    i = pl.program_id(0)
    tm = h_ref.shape[0]
    h = h_ref[...]
    hb = h.astype(MXU_DTYPE)

    @pl.when(i % tiles_per_seq == 0)
    def _():
        gate_ref[0:CONV_HALO, :] = jnp.zeros((CONV_HALO, D_FF), jnp.float32)

    ffn = jnp.zeros((tm, D_MODEL), jnp.float32)
    for f0 in range(0, D_FF, FF_CHUNK):
        f1 = f0 + FF_CHUNK
        gate_ref[CONV_HALO:, f0:f1] = _dot(hb, wup_ref[:, f0:f1])
        val = _dot(hb, wup_ref[:, D_FF + f0:D_FF + f1])
        conv = cb_ref[:, f0:f1]
        for j in range(CONV_W):
            off = CONV_HALO - (CONV_W - 1) + j
            conv = conv + cw_ref[j:j + 1, f0:f1] * gate_ref[off:off + tm, f0:f1]
        act = conv / (1.0 + jnp.exp(-conv)) * val
        ffn = ffn + _dot(act.astype(MXU_DTYPE), wdn_ref[f0:f1, :])
    gate_ref[0:CONV_HALO, :] = gate_ref[tm:tm + CONV_HALO, :]
    out_ref[...] = _layer_norm_rows(ALPHA * h + ffn, g_ref[...], b_ref[...])


def _conv_gated_mlp(h, lw, seq, tm):
    t = h.shape[0]
    row = pl.BlockSpec((tm, D_MODEL), lambda i: (i, 0))
    weights = [lw["wup"], lw["conv_w"], lw["conv_b"], lw["wdn"], lw["ln2_g"], lw["ln2_b"]]
    return pl.pallas_call(
        functools.partial(_ffn_kernel, tiles_per_seq=seq // tm), grid=(t // tm,),
        in_specs=[row] + [_const_spec(w.shape) for w in weights],
        out_specs=row, out_shape=jax.ShapeDtypeStruct((t, D_MODEL), jnp.float32),
        scratch_shapes=[pltpu.VMEM((tm + CONV_HALO, D_FF), jnp.float32)],
        compiler_params=_params(("arbitrary",)), name="conv_gated_mlp",
    )(h, *weights)


def _layer_weights(l, w_in, b_f, kv_norm_g, w_ukv, w_gate, w_br_a, w_br_b, w_br_c, w_out, ln1_g, ln1_b,
                   w_up, conv_w, conv_b, w_down, ln2_g, ln2_b):
    widths = (HA * DA, HA * DA, HA * DA, HA, HB * DB, KV_LATENT, HI * DI, DI, HI,
              HC * 2 * DC, HC * 2 * DC, HC * DCV)
    offs = [0]
    for n in widths:
        offs.append(offs[-1] + n)
    col = lambda k: w_in[l][:, offs[k]:offs[k + 1]]
    aq, ak, av, af, bq, bc, biq, bik, biw, cq, ck, cv = (col(k) for k in range(len(widths)))
    pad = jnp.zeros((D_MODEL, LANES - HI - HA - DI), w_in.dtype)
    mx = lambda a: a.astype(MXU_DTYPE)
    vec = lambda a: a.reshape(1, -1).astype(jnp.float32)
    bf = jnp.zeros((1, LANES), jnp.float32).at[0, SMALL_F_OFF:SMALL_F_OFF + HA].set(b_f[l])
    return dict(
        wa=mx(jnp.concatenate([aq, ak, av], axis=1)), wbq=mx(bq), wbc=mx(bc), wiq=mx(biq),
        wc=mx(jnp.concatenate([cq, ck, cv], axis=1)),
        wsm=mx(jnp.concatenate([biw, af, pad, bik], axis=1)), bf=bf, kvg=vec(kv_norm_g[l]),
        wukv=mx(w_ukv[l]), wg=mx(w_gate[l]), wbr_a=mx(w_br_a[l]), wbr_b=mx(w_br_b[l]),
        wbr_c=mx(w_br_c[l]), wo=mx(w_out[l]), ln1_g=vec(ln1_g[l]), ln1_b=vec(ln1_b[l]),
        wup=mx(w_up[l]), conv_w=conv_w[l].astype(jnp.float32), conv_b=vec(conv_b[l]),
        wdn=mx(w_down[l]), ln2_g=vec(ln2_g[l]), ln2_b=vec(ln2_b[l]))


def _tile(n, want):
    t = min(n, want)
    assert n % t == 0, (n, t)
    return t


def kernel(x, positions, ln_in_g, ln_in_b, w_in, b_f, kv_norm_g, w_ukv, lam_qk, diff_norm_g, w_gate,
           w_br_a, w_br_b, w_br_c, w_out, ln1_g, ln1_b, w_up, conv_w, conv_b, w_down, ln2_g, ln2_b):
    b, s, d = x.shape
    assert d == D_MODEL and s % LANES == 0
    t = b * s
    depth = w_in.shape[0]
    top_k = min(TOPK_MAX, s // 4)
    tm = _tile(s, 512)
    tq = _tile(s, 256)
    tk_dsa = _tile(s, 512)

    h = _input_layer_norm(x.reshape(t, d), ln_in_g, ln_in_b, tm)
    pos_lanes = jnp.broadcast_to(positions.reshape(t, 1).astype(jnp.float32), (t, LANES))
    tables = _rope_tables(pos_lanes, tm)

    for l in range(depth):
        lam_init = 0.8 - 0.6 * math.exp(-0.3 * l)
        lw = _layer_weights(l, w_in, b_f, kv_norm_g, w_ukv, w_gate, w_br_a, w_br_b, w_br_c, w_out,
                            ln1_g, ln1_b, w_up, conv_w, conv_b, w_down, ln2_g, ln2_b)
        (qa, ka, va_t, qb, kb, vb_t, qi, ki2, sm, qc, kc, vc_t) = _projections(h, tables, lw, s, tm)
        seq3 = lambda a: a.reshape(b, s, a.shape[-1])
        o_a = _fox_attention(seq3(qa), seq3(ka), va_t, tq, FOX_HEADS)
        o_b = _dsa_attention(seq3(qi), seq3(sm), seq3(ki2), seq3(qb), seq3(kb), vb_t,
                             top_k, tq, tk_dsa)
        o_c = _diff_attention(seq3(qc), seq3(kc), vc_t, lam_qk[l].astype(jnp.float32),
                              diff_norm_g[l].astype(jnp.float32), lam_init, tq, DIFF_HEADS)
        h = _merge(h, o_a.reshape(t, -1), o_b.reshape(t, -1), o_c.reshape(t, -1), lw, tm)
        h = _conv_gated_mlp(h, lw, s, tm)
    return h.reshape(b, s, d)
```
